```python
import math
import jax, jax.numpy as jnp
from jax import lax
import numpy as np

D_MODEL = 1024
BATCH = 32
SEQ = 2048
DEPTH = 2

HEAD_DIM = 64
N_HEADS = D_MODEL // HEAD_DIM
D_FF = 4 * D_MODEL
N_MIXERS = 2
MOBA_BLOCK = 256
MOBA_TOPK = 3
MOBA_Q_CHUNK = 8
SB_Q_BLOCK = 128
REL_BUCKETS = 32
REL_MAX_DIST = 128
LN_EPS = 1e-5
DEEPNORM_ALPHA = (2.0 * DEPTH) ** 0.25
DEEPNORM_BETA = (8.0 * DEPTH) ** -0.25

kernel_name = "moba_stickbreaking_deepnorm_hybrid"


def layer_norm(x, g, b):
    xf = x.astype(jnp.float32)
    mu = jnp.mean(xf, axis=-1, keepdims=True)
    var = jnp.mean(jnp.square(xf - mu), axis=-1, keepdims=True)
    y = (xf - mu) * lax.rsqrt(var + LN_EPS)
    return (y * g.astype(jnp.float32) + b.astype(jnp.float32)).astype(x.dtype)


def t5_bucket(dist):
    n = jnp.maximum(dist, 0)
    max_exact = REL_BUCKETS // 2
    nf = jnp.maximum(n, 1).astype(jnp.float32)
    large = max_exact + (jnp.log(nf / max_exact) / math.log(REL_MAX_DIST / max_exact)
                         * (REL_BUCKETS - max_exact)).astype(jnp.int32)
    large = jnp.minimum(large, REL_BUCKETS - 1)
    return jnp.where(n < max_exact, n, large)


def moba_attention(q, k, v, rel_bias):
    B, H, S, dh = q.shape
    nb = -(-S // MOBA_BLOCK)
    s_pad = nb * MOBA_BLOCK
    topk = min(MOBA_TOPK, nb)
    pad = ((0, 0), (0, 0), (0, s_pad - S), (0, 0))
    k_pad = jnp.pad(k, pad)
    v_pad = jnp.pad(v, pad)
    k_blk = k_pad.reshape(B, H, nb, MOBA_BLOCK, dh)
    v_blk = v_pad.reshape(B, H, nb, MOBA_BLOCK, dh)
    k_mean = jnp.mean(k_blk.astype(jnp.float32), axis=3)

    q_blk_id = jnp.arange(S) // MOBA_BLOCK
    gate = jnp.einsum('bhsd,bhnd->bhsn', q.astype(jnp.float32), k_mean)
    past = jnp.arange(nb)[None, :] < q_blk_id[:, None]
    gate = jnp.where(past, gate, -jnp.inf)
    _, sel = lax.top_k(gate, topk)
    sel_valid = sel < q_blk_id[:, None]

    nc = S // MOBA_Q_CHUNK

    def to_chunks(a):
        a = a.reshape((B, H, nc, MOBA_Q_CHUNK) + a.shape[3:])
        return jnp.moveaxis(a, 2, 0)

    table_t = rel_bias.T.astype(jnp.float32)
    head_ix = jnp.arange(H)[None, :, None, None, None]
    gather_blocks = jax.vmap(jax.vmap(lambda blk, ix: blk[ix]))
    scale = dh ** -0.5
    w_ar = jnp.arange(MOBA_BLOCK)

    def chunk_fn(args):
        qc, selc, validc, c = args
        t = c * MOBA_Q_CHUNK + jnp.arange(MOBA_Q_CHUNK)
        kg = gather_blocks(k_blk, selc)
        vg = gather_blocks(v_blk, selc)
        key_pos_g = selc[..., None] * MOBA_BLOCK + w_ar
        bias_g = table_t[head_ix, t5_bucket(t[:, None, None] - key_pos_g)]
        s_g = jnp.einsum('bhqd,bhqkwd->bhqkw', qc, kg).astype(jnp.float32) * scale + bias_g
        s_g = jnp.where(validc[..., None], s_g, -jnp.inf)
        start = ((c * MOBA_Q_CHUNK) // MOBA_BLOCK) * MOBA_BLOCK
        ko = lax.dynamic_slice_in_dim(k_pad, start, MOBA_BLOCK, axis=2)
        vo = lax.dynamic_slice_in_dim(v_pad, start, MOBA_BLOCK, axis=2)
        rel_o = t[:, None] - (start + w_ar)[None, :]
        bias_o = table_t[:, t5_bucket(rel_o)]
        s_o = jnp.einsum('bhqd,bhwd->bhqw', qc, ko).astype(jnp.float32) * scale + bias_o
        s_o = jnp.where(rel_o >= 0, s_o, -jnp.inf)
        logits = jnp.concatenate(
            [s_g.reshape(B, H, MOBA_Q_CHUNK, topk * MOBA_BLOCK), s_o], axis=-1)
        p = jax.nn.softmax(logits, axis=-1).astype(v.dtype)
        p_g = p[..., :topk * MOBA_BLOCK].reshape(B, H, MOBA_Q_CHUNK, topk, MOBA_BLOCK)
        p_o = p[..., topk * MOBA_BLOCK:]
        return (jnp.einsum('bhqkw,bhqkwd->bhqd', p_g, vg)
                + jnp.einsum('bhqw,bhwd->bhqd', p_o, vo))

    out = lax.map(chunk_fn, (to_chunks(q), to_chunks(sel), to_chunks(sel_valid),
                             jnp.arange(nc)))
    return jnp.moveaxis(out, 0, 2).reshape(B, H, S, dh)


def stick_breaking_attention(q, k, v):
    B, H, S, dh = q.shape
    scale = dh ** -0.5
    outs = []
    for t0 in range(0, S, SB_Q_BLOCK):
        end = t0 + SB_Q_BLOCK
        qb = q[:, :, t0:end]
        kk = k[:, :, :end]
        vv = v[:, :, :end]
        z = jnp.einsum('bhqd,bhsd->bhqs', qb, kk).astype(jnp.float32) * scale
        t = t0 + jnp.arange(SB_Q_BLOCK)[:, None]
        s = jnp.arange(end)[None, :]
        causal = s < t
        log_keep = jnp.where(causal, jax.nn.log_sigmoid(-z), 0.0)
        after = lax.cumsum(log_keep, axis=3, reverse=True) - log_keep
        a = jnp.where(causal, jnp.exp(jax.nn.log_sigmoid(z) + after), 0.0)
        outs.append(jnp.einsum('bhqs,bhsd->bhqd', a.astype(v.dtype), vv))
    return jnp.concatenate(outs, axis=2)


def setup_inputs(seed: int = 0) -> dict:
    key = jax.random.key(seed)
    ks = jax.random.split(key, 10)
    D = D_MODEL
    x = jax.random.normal(ks[0], (BATCH, SEQ, D), jnp.float32)
    rel_bias = 0.5 * jax.random.normal(ks[1], (REL_BUCKETS, N_HEADS), jnp.float32)
    col_scale = jnp.concatenate([jnp.ones((2 * D,), jnp.float32),
                                 jnp.full((D,), DEEPNORM_BETA, jnp.float32)])
    w_qkv = jax.random.normal(ks[2], (DEPTH, D, 3 * D), jnp.float32) * (D ** -0.5) * col_scale
    w_o = jax.random.normal(ks[3], (DEPTH, D, D), jnp.float32) * (D ** -0.5) * DEEPNORM_BETA
    ln_mix_g = 1.0 + 0.02 * jax.random.normal(ks[4], (DEPTH, D), jnp.float32)
    ln_mix_b = 0.02 * jax.random.normal(ks[5], (DEPTH, D), jnp.float32)
    w_up = jax.random.normal(ks[6], (DEPTH, D, D_FF), jnp.float32) * (D ** -0.5) * DEEPNORM_BETA
    w_down = jax.random.normal(ks[7], (DEPTH, D_FF, D), jnp.float32) * (D_FF ** -0.5) * DEEPNORM_BETA
    ln_ffn_g = 1.0 + 0.02 * jax.random.normal(ks[8], (DEPTH, D), jnp.float32)
    ln_ffn_b = 0.02 * jax.random.normal(ks[9], (DEPTH, D), jnp.float32)
    return {"x": x, "rel_bias": rel_bias, "w_qkv": w_qkv, "w_o": w_o,
            "ln_mix_g": ln_mix_g, "ln_mix_b": ln_mix_b, "w_up": w_up, "w_down": w_down,
            "ln_ffn_g": ln_ffn_g, "ln_ffn_b": ln_ffn_b}


def reference(x, rel_bias, w_qkv, w_o, ln_mix_g, ln_mix_b, w_up, w_down, ln_ffn_g, ln_ffn_b):
    B, S, D = x.shape
    h = x
    for i in range(DEPTH):
        qkv = (h @ w_qkv[i]).reshape(B, S, 3, N_HEADS, HEAD_DIM)
        qkv = jnp.transpose(qkv, (2, 0, 3, 1, 4))
        q, k, v = qkv[0], qkv[1], qkv[2]
        if i % N_MIXERS == 0:
            o = moba_attention(q, k, v, rel_bias)
        else:
            o = stick_breaking_attention(q, k, v)
        o = jnp.transpose(o, (0, 2, 1, 3)).reshape(B, S, D) @ w_o[i]
        h = layer_norm(DEEPNORM_ALPHA * h + o, ln_mix_g[i], ln_mix_b[i])
        u = jnp.square(jax.nn.relu(h @ w_up[i]))
        h = layer_norm(DEEPNORM_ALPHA * h + u @ w_down[i], ln_ffn_g[i], ln_ffn_b[i])
    return h
```

```python
import functools
import math

import numpy as np
import jax
import jax.numpy as jnp
from jax import lax
from jax.experimental import pallas as pl
from jax.experimental.pallas import tpu as pltpu

HEAD_DIM = 64
HEADS_PER_STEP = 2
LANES = 128
MOBA_BLOCK = 256
MOBA_TOPK = 3
REL_BUCKETS = 32
REL_MAX_DIST = 128
LN_EPS = 1e-5
TOKEN_TILE = 512
FF_CHUNK = 1024
VMEM_LIMIT_BYTES = 48 * 1024 * 1024

_F32 = jnp.float32
_BF16 = jnp.bfloat16
_NEG_INF = float("-inf")


def _t5_bucket_upper_bounds():
    n = np.arange(0, 4 * REL_MAX_DIST)
    max_exact = REL_BUCKETS // 2
    nf = np.maximum(n, 1).astype(np.float64)
    large = max_exact + (np.log(nf / max_exact) / math.log(REL_MAX_DIST / max_exact)
                         * (REL_BUCKETS - max_exact)).astype(np.int64)
    large = np.minimum(large, REL_BUCKETS - 1)
    bucket = np.where(n < max_exact, n, large)
    assert bucket[-1] == REL_BUCKETS - 1 and np.all(np.diff(bucket) >= 0)
    return tuple(int(n[bucket == b].max()) for b in range(REL_BUCKETS - 1))


_BUCKET_UB = _t5_bucket_upper_bounds()
assert _BUCKET_UB[-1] < MOBA_BLOCK


def _layer_norm(y, g, b):
    mu = jnp.mean(y, axis=-1, keepdims=True)
    yc = y - mu
    var = jnp.mean(yc * yc, axis=-1, keepdims=True)
    return yc * lax.rsqrt(var + LN_EPS) * g + b


def _qkv_kernel(x_ref, w_ref, o_ref, *, d_model, q_scale):
    xb = x_ref[...].astype(_BF16)
    for c in range(3):
        cols = slice(c * d_model, (c + 1) * d_model)
        acc = jnp.dot(xb, w_ref[:, cols], preferred_element_type=_F32)
        if c == 0:
            acc = acc * q_scale
        o_ref[:, cols] = acc.astype(_BF16)


def _qkv_proj(h2d, w_qkv_bf16):
    t, d = h2d.shape
    tm = min(TOKEN_TILE, t)
    assert t % tm == 0
    return pl.pallas_call(
        functools.partial(_qkv_kernel, d_model=d, q_scale=HEAD_DIM ** -0.5),
        grid=(t // tm,),
        in_specs=[pl.BlockSpec((tm, d), lambda i: (i, 0)),
                  pl.BlockSpec((d, 3 * d), lambda i: (0, 0), pipeline_mode=pl.Buffered(1))],
        out_specs=pl.BlockSpec((tm, 3 * d), lambda i: (i, 0)),
        out_shape=jax.ShapeDtypeStruct((t, 3 * d), _BF16),
        compiler_params=pltpu.CompilerParams(dimension_semantics=("arbitrary",),
                                             vmem_limit_bytes=VMEM_LIMIT_BYTES),
        name="qkv_proj",
    )(h2d, w_qkv_bf16)


def _oproj_ln_kernel(o_ref, h_ref, w_ref, g_ref, b_ref, out_ref, *, alpha):
    acc = jnp.dot(o_ref[...], w_ref[...], preferred_element_type=_F32)
    out_ref[...] = _layer_norm(alpha * h_ref[...] + acc, g_ref[...], b_ref[...])


def _oproj_ln(o2d, h2d, w_o_bf16, g, b, alpha):
    t, d = h2d.shape
    tm = min(TOKEN_TILE, t)
    assert t % tm == 0
    return pl.pallas_call(
        functools.partial(_oproj_ln_kernel, alpha=alpha),
        grid=(t // tm,),
        in_specs=[pl.BlockSpec((tm, d), lambda i: (i, 0)),
                  pl.BlockSpec((tm, d), lambda i: (i, 0)),
                  pl.BlockSpec((d, d), lambda i: (0, 0), pipeline_mode=pl.Buffered(1)),
                  pl.BlockSpec((1, d), lambda i: (0, 0)),
                  pl.BlockSpec((1, d), lambda i: (0, 0))],
        out_specs=pl.BlockSpec((tm, d), lambda i: (i, 0)),
        out_shape=jax.ShapeDtypeStruct((t, d), _F32),
        compiler_params=pltpu.CompilerParams(dimension_semantics=("arbitrary",),
                                             vmem_limit_bytes=VMEM_LIMIT_BYTES),
        name="oproj_ln",
    )(o2d, h2d, w_o_bf16, g.reshape(1, d), b.reshape(1, d))


def _mlp_ln_kernel(h_ref, wu_ref, wd_ref, g_ref, b_ref, out_ref, *, alpha, d_ff, fc):
    h = h_ref[...]
    hb = h.astype(_BF16)
    acc = alpha * h
    for c in range(d_ff // fc):
        u = jnp.dot(hb, wu_ref[:, c * fc:(c + 1) * fc], preferred_element_type=_F32)
        u = jnp.square(jnp.maximum(u, 0.0)).astype(_BF16)
        acc = acc + jnp.dot(u, wd_ref[c * fc:(c + 1) * fc, :], preferred_element_type=_F32)
    out_ref[...] = _layer_norm(acc, g_ref[...], b_ref[...])


def _mlp_ln(h2d, w_up_bf16, w_down_bf16, g, b, alpha):
    t, d = h2d.shape
    d_ff = w_up_bf16.shape[1]
    tm = min(TOKEN_TILE, t)
    fc = min(FF_CHUNK, d_ff)
    assert t % tm == 0 and d_ff % fc == 0
    return pl.pallas_call(
        functools.partial(_mlp_ln_kernel, alpha=alpha, d_ff=d_ff, fc=fc),
        grid=(t // tm,),
        in_specs=[pl.BlockSpec((tm, d), lambda i: (i, 0)),
                  pl.BlockSpec((d, d_ff), lambda i: (0, 0), pipeline_mode=pl.Buffered(1)),
                  pl.BlockSpec((d_ff, d), lambda i: (0, 0), pipeline_mode=pl.Buffered(1)),
                  pl.BlockSpec((1, d), lambda i: (0, 0)),
                  pl.BlockSpec((1, d), lambda i: (0, 0))],
        out_specs=pl.BlockSpec((tm, d), lambda i: (i, 0)),
        out_shape=jax.ShapeDtypeStruct((t, d), _F32),
        compiler_params=pltpu.CompilerParams(dimension_semantics=("arbitrary",),
                                             vmem_limit_bytes=VMEM_LIMIT_BYTES),
        name="mlp_ln",
    )(h2d, w_up_bf16, w_down_bf16, g.reshape(1, d), b.reshape(1, d))


def _bias_kernel(tab_ref, bias_ref, cfar_ref):
    h = pl.program_id(0)
    w = MOBA_BLOCK
    key = lax.broadcasted_iota(jnp.int32, (w, w), 0)
    qry = lax.broadcasted_iota(jnp.int32, (w, w), 1)
    far = tab_ref[REL_BUCKETS - 1, h]
    for d in range(2):
        dist = qry - key + d * w
        val = jnp.full((w, w), far, _F32)
        for bkt in range(REL_BUCKETS - 2, -1, -1):
            val = jnp.where(dist <= _BUCKET_UB[bkt], tab_ref[bkt, h], val)
        if d == 0:
            val = jnp.where(dist >= 0, val, _NEG_INF)
        bias_ref[0, d] = val
    cfar_ref[0] = jnp.full((8, w), far, _F32)


def _bias_tiles(rel_bias):
    n_heads = rel_bias.shape[1]
    w = MOBA_BLOCK
    return pl.pallas_call(
        _bias_kernel,
        grid=(n_heads,),
        in_specs=[pl.BlockSpec(memory_space=pltpu.SMEM)],
        out_specs=[pl.BlockSpec((1, 2, w, w), lambda h: (h, 0, 0, 0)),
                   pl.BlockSpec((1, 8, w), lambda h: (h, 0, 0))],
        out_shape=[jax.ShapeDtypeStruct((n_heads, 2, w, w), _F32),
                   jax.ShapeDtypeStruct((n_heads, 8, w), _F32)],
        compiler_params=pltpu.CompilerParams(dimension_semantics=("arbitrary",)),
        name="t5_bias_tiles",
    )(rel_bias.astype(_F32))


def _head_rows(hh):
    row = lax.broadcasted_iota(jnp.int32, (LANES, MOBA_BLOCK), 0)
    return (row // HEAD_DIM) == hh


def _store_transposed_operands(q_ref, v_ref, qT_ref, vT_ref, hh, nb):
    w = MOBA_BLOCK
    mine = _head_rows(hh)
    for i in range(nb):
        q_t = q_ref[0, i * w:(i + 1) * w, :].astype(_F32).T
        qT_ref[i] = jnp.where(mine, q_t, 0.0).astype(_BF16)

    @pl.when(hh == 0)
    def _():
        for i in range(nb):
            vT_ref[i] = v_ref[0, i * w:(i + 1) * w, :].astype(_F32).T.astype(_BF16)


def _merge_and_store(o_ref, oT_ref, res, hh, nb):
    w = MOBA_BLOCK

    @pl.when(hh == 0)
    def _():
        for i in range(nb):
            oT_ref[i] = res[i]

    @pl.when(hh == HEADS_PER_STEP - 1)
    def _():
        mine = _head_rows(hh)
        for i in range(nb):
            both = jnp.where(mine, res[i], oT_ref[i])
            o_ref[0, i * w:(i + 1) * w, :] = both.T.astype(_BF16)


def _moba_kernel(q_ref, k_ref, v_ref, bias_ref, cfar_ref, o_ref,
                 qT_ref, vT_ref, kparts_ref, oT_ref, *, nb, topk):
    w = MOBA_BLOCK
    hh = pl.program_id(2)
    _store_transposed_operands(q_ref, v_ref, qT_ref, vT_ref, hh, nb)

    @pl.when(hh == 0)
    def _():
        rows = []
        for j in range(nb):
            kj = k_ref[0, j * w:(j + 1) * w, :].astype(_F32)
            rows.append(jnp.mean(kj, axis=0, keepdims=True))
        kmean = jnp.concatenate(rows + [jnp.zeros((16 - nb, LANES), _F32)], axis=0)
        p0 = kmean.astype(_BF16)
        r1 = kmean - p0.astype(_F32)
        p1 = r1.astype(_BF16)
        p2 = (r1 - p1.astype(_F32)).astype(_BF16)
        kparts_ref[0:16, :] = p0
        kparts_ref[16:32, :] = p1
        kparts_ref[32:48, :] = p2

    jidx = lax.broadcasted_iota(jnp.int32, (8, w), 0)
    cfar = cfar_ref[0][0:1, :]
    res = []
    for i in range(nb):
        q_t = qT_ref[i]
        if i > 0:
            g3 = jnp.dot(kparts_ref[...], q_t, preferred_element_type=_F32)
            gate = g3[0:8] + g3[16:24] + g3[32:40]
            cnt = jnp.zeros((8, w), jnp.int32)
            for jp in range(i):
                rowv = gate[jp:jp + 1, :]
                beats = (rowv > gate) | ((rowv == gate) & (jp < jidx))
                cnt = cnt + jnp.where(beats, 1, 0)
            sel = (cnt < topk) & (jidx < i)
            addrow = jnp.where(sel, 0.0, _NEG_INF)
            addrow_far = addrow + cfar
        m = l = acc = None
        for j in range(i, -1, -1):
            d = i - j
            s = jnp.dot(k_ref[0, j * w:(j + 1) * w, :], q_t, preferred_element_type=_F32)
            if d == 0:
                s = s + bias_ref[0, 0]
            elif d == 1:
                s = s + bias_ref[0, 1] + addrow[j:j + 1, :]
            else:
                s = s + addrow_far[j:j + 1, :]
            mt = jnp.max(s, axis=0, keepdims=True)
            if d == 0:
                m = mt
                p = jnp.exp(s - m)
                l = jnp.sum(p, axis=0, keepdims=True)
                acc = jnp.dot(vT_ref[j], p.astype(_BF16), preferred_element_type=_F32)
            else:
                m_new = jnp.maximum(m, mt)
                a = jnp.exp(m - m_new)
                p = jnp.exp(s - m_new)
                l = a * l + jnp.sum(p, axis=0, keepdims=True)
                acc = a * acc + jnp.dot(vT_ref[j], p.astype(_BF16), preferred_element_type=_F32)
                m = m_new
        res.append(acc * (1.0 / l))
    _merge_and_store(o_ref, oT_ref, res, hh, nb)


def _attention_call(kernel, qkv3, extra_inputs, extra_specs, scratch, name):
    b, s, d3 = qkv3.shape
    d = d3 // 3
    w = MOBA_BLOCK
    assert s % w == 0 and d % LANES == 0
    nb = s // w
    assert nb <= 8
    n_lane_blocks = d // LANES
    qkv_specs = [pl.BlockSpec((1, s, LANES), lambda hp, bi, hh, off=off: (bi, 0, off + hp))
                 for off in (0, n_lane_blocks, 2 * n_lane_blocks)]
    return pl.pallas_call(
        functools.partial(kernel, nb=nb),
        grid=(n_lane_blocks, b, HEADS_PER_STEP),
        in_specs=qkv_specs + extra_specs,
        out_specs=pl.BlockSpec((1, s, LANES), lambda hp, bi, hh: (bi, 0, hp)),
        out_shape=jax.ShapeDtypeStruct((b, s, d), _BF16),
        scratch_shapes=[pltpu.VMEM((nb, LANES, w), _BF16),
                        pltpu.VMEM((nb, LANES, w), _BF16)]
                       + scratch
                       + [pltpu.VMEM((nb, LANES, w), _F32)],
        compiler_params=pltpu.CompilerParams(
            dimension_semantics=("arbitrary", "arbitrary", "arbitrary"),
            vmem_limit_bytes=VMEM_LIMIT_BYTES),
        name=name,
    )(qkv3, qkv3, qkv3, *extra_inputs)


def _moba_attention(qkv3, bias_tiles, cfar):
    w = MOBA_BLOCK
    nb = qkv3.shape[1] // w
    topk = min(MOBA_TOPK, nb)
    extra_specs = [
        pl.BlockSpec((1, 2, w, w), lambda hp, bi, hh: (hp * HEADS_PER_STEP + hh, 0, 0, 0)),
        pl.BlockSpec((1, 8, w), lambda hp, bi, hh: (hp * HEADS_PER_STEP + hh, 0, 0)),
    ]
    return _attention_call(functools.partial(_moba_kernel, topk=topk), qkv3,
                           [bias_tiles, cfar], extra_specs,
                           [pltpu.VMEM((48, LANES), _BF16)], "moba_attention")


def _sb_kernel(q_ref, k_ref, v_ref, o_ref, qT_ref, vT_ref, oT_ref, *, nb):
    w = MOBA_BLOCK
    hh = pl.program_id(2)
    _store_transposed_operands(q_ref, v_ref, qT_ref, vT_ref, hh, nb)

    key = lax.broadcasted_iota(jnp.int32, (w, w), 0)
    col = lax.broadcasted_iota(jnp.int32, (w, w), 1)
    causal = key < col
    upper = jnp.where(col > key, 1.0, 0.0).astype(_BF16)
    res = []
    for i in range(nb):
        q_t = qT_ref[i]
        r = acc = None
        for j in range(i, -1, -1):
            z = jnp.dot(k_ref[0, j * w:(j + 1) * w, :], q_t, preferred_element_type=_F32)
            lk = -(jnp.maximum(z, 0.0) + jnp.log1p(jnp.exp(-jnp.abs(z))))
            if j == i:
                lk = jnp.where(causal, lk, 0.0)
            hi = lk.astype(_BF16)
            lo = (lk - hi.astype(_F32)).astype(_BF16)
            after = (jnp.dot(upper, hi, preferred_element_type=_F32)
                     + jnp.dot(upper, lo, preferred_element_type=_F32))
            log_a = z + lk + after
            if j != i:
                log_a = log_a + r
            a = jnp.exp(log_a)
            if j == i:
                a = jnp.where(causal, a, 0.0)
            pv = jnp.dot(vT_ref[j], a.astype(_BF16), preferred_element_type=_F32)
            tile_total = after[0:1, :] + lk[0:1, :]
            if j == i:
                acc, r = pv, tile_total
            else:
                acc, r = acc + pv, r + tile_total
        res.append(acc)
    _merge_and_store(o_ref, oT_ref, res, hh, nb)


def _sb_attention(qkv3):
    return _attention_call(_sb_kernel, qkv3, [], [], [], "stick_breaking_attention")


def kernel(x, rel_bias, w_qkv, w_o, ln_mix_g, ln_mix_b, w_up, w_down, ln_ffn_g, ln_ffn_b):
    b, s, d = x.shape
    depth = w_qkv.shape[0]
    alpha = (2.0 * depth) ** 0.25
    assert d % (HEADS_PER_STEP * HEAD_DIM) == 0
    bias_tiles, cfar = _bias_tiles(rel_bias)
    h = x.reshape(b * s, d)
    for i in range(depth):
        qkv = _qkv_proj(h, w_qkv[i].astype(_BF16)).reshape(b, s, 3 * d)
        if i % 2 == 0:
            o = _moba_attention(qkv, bias_tiles, cfar)
        else:
            o = _sb_attention(qkv)
        h = _oproj_ln(o.reshape(b * s, d), h, w_o[i].astype(_BF16),
                      ln_mix_g[i], ln_mix_b[i], alpha)
        h = _mlp_ln(h, w_up[i].astype(_BF16), w_down[i].astype(_BF16),
                    ln_ffn_g[i], ln_ffn_b[i], alpha)
    return h.reshape(b, s, d)
```

```python
import functools
import math

import numpy as np
import jax
import jax.numpy as jnp
from jax import lax
from jax.experimental import pallas as pl
from jax.experimental.pallas import tpu as pltpu

HEAD_DIM = 64
HEADS_PER_STEP = 2
LANES = 128
MOBA_BLOCK = 256
MOBA_TOPK = 3
REL_BUCKETS = 32
REL_MAX_DIST = 128
LN_EPS = 1e-5
TOKEN_TILE = 512
FF_CHUNK = 1024
VMEM_LIMIT_BYTES = 48 * 1024 * 1024
MOBA_LOOKAHEAD = 4
SB_LOOKAHEAD = 2
LOG2E = math.log2(math.e)
LN2 = math.log(2.0)

_F32 = jnp.float32
_BF16 = jnp.bfloat16
_NEG_INF = float("-inf")


def _t5_bucket_upper_bounds():
    n = np.arange(0, 4 * REL_MAX_DIST)
    max_exact = REL_BUCKETS // 2
    nf = np.maximum(n, 1).astype(np.float64)
    large = max_exact + (np.log(nf / max_exact) / math.log(REL_MAX_DIST / max_exact)
                         * (REL_BUCKETS - max_exact)).astype(np.int64)
    large = np.minimum(large, REL_BUCKETS - 1)
    bucket = np.where(n < max_exact, n, large)
    assert bucket[-1] == REL_BUCKETS - 1 and np.all(np.diff(bucket) >= 0)
    return tuple(int(n[bucket == b].max()) for b in range(REL_BUCKETS - 1))


_BUCKET_UB = _t5_bucket_upper_bounds()
assert _BUCKET_UB[-1] < MOBA_BLOCK


def _layer_norm(y, g, b):
    mu = jnp.mean(y, axis=-1, keepdims=True)
    yc = y - mu
    var = jnp.mean(yc * yc, axis=-1, keepdims=True)
    return yc * lax.rsqrt(var + LN_EPS) * g + b


def _qkv_kernel(x_ref, w_ref, o_ref, *, d_model, q_scale):
    xb = x_ref[...].astype(_BF16)
    for c in range(3):
        cols = slice(c * d_model, (c + 1) * d_model)
        acc = jnp.dot(xb, w_ref[:, cols], preferred_element_type=_F32)
        if c == 0:
            acc = acc * q_scale
        o_ref[:, cols] = acc.astype(_BF16)


def _qkv_proj(h2d, w_qkv_bf16):
    t, d = h2d.shape
    tm = min(TOKEN_TILE, t)
    assert t % tm == 0
    return pl.pallas_call(
        functools.partial(_qkv_kernel, d_model=d, q_scale=LOG2E * HEAD_DIM ** -0.5),
        grid=(t // tm,),
        in_specs=[pl.BlockSpec((tm, d), lambda i: (i, 0)),
                  pl.BlockSpec((d, 3 * d), lambda i: (0, 0), pipeline_mode=pl.Buffered(1))],
        out_specs=pl.BlockSpec((tm, 3 * d), lambda i: (i, 0)),
        out_shape=jax.ShapeDtypeStruct((t, 3 * d), _BF16),
        compiler_params=pltpu.CompilerParams(dimension_semantics=("arbitrary",),
                                             vmem_limit_bytes=VMEM_LIMIT_BYTES),
        name="qkv_proj",
    )(h2d, w_qkv_bf16)


def _oproj_ln_kernel(o_ref, h_ref, w_ref, g_ref, b_ref, out_ref, *, alpha):
    acc = jnp.dot(o_ref[...], w_ref[...], preferred_element_type=_F32)
    out_ref[...] = _layer_norm(alpha * h_ref[...] + acc, g_ref[...], b_ref[...])


def _oproj_ln(o2d, h2d, w_o_bf16, g, b, alpha):
    t, d = h2d.shape
    tm = min(TOKEN_TILE, t)
    assert t % tm == 0
    return pl.pallas_call(
        functools.partial(_oproj_ln_kernel, alpha=alpha),
        grid=(t // tm,),
        in_specs=[pl.BlockSpec((tm, d), lambda i: (i, 0)),
                  pl.BlockSpec((tm, d), lambda i: (i, 0)),
                  pl.BlockSpec((d, d), lambda i: (0, 0), pipeline_mode=pl.Buffered(1)),
                  pl.BlockSpec((1, d), lambda i: (0, 0)),
                  pl.BlockSpec((1, d), lambda i: (0, 0))],
        out_specs=pl.BlockSpec((tm, d), lambda i: (i, 0)),
        out_shape=jax.ShapeDtypeStruct((t, d), _F32),
        compiler_params=pltpu.CompilerParams(dimension_semantics=("arbitrary",),
                                             vmem_limit_bytes=VMEM_LIMIT_BYTES),
        name="oproj_ln",
    )(o2d, h2d, w_o_bf16, g.reshape(1, d), b.reshape(1, d))


def _mlp_ln_kernel(h_ref, wu_ref, wd_ref, g_ref, b_ref, out_ref, *, alpha, d_ff, fc):
    h = h_ref[...]
    hb = h.astype(_BF16)
    acc = alpha * h
    for c in range(d_ff // fc):
        u = jnp.dot(hb, wu_ref[:, c * fc:(c + 1) * fc], preferred_element_type=_F32)
        u = jnp.square(jnp.maximum(u, 0.0)).astype(_BF16)
        acc = acc + jnp.dot(u, wd_ref[c * fc:(c + 1) * fc, :], preferred_element_type=_F32)
    out_ref[...] = _layer_norm(acc, g_ref[...], b_ref[...])


def _mlp_ln(h2d, w_up_bf16, w_down_bf16, g, b, alpha):
    t, d = h2d.shape
    d_ff = w_up_bf16.shape[1]
    tm = min(TOKEN_TILE, t)
    fc = min(FF_CHUNK, d_ff)
    assert t % tm == 0 and d_ff % fc == 0
    return pl.pallas_call(
        functools.partial(_mlp_ln_kernel, alpha=alpha, d_ff=d_ff, fc=fc),
        grid=(t // tm,),
        in_specs=[pl.BlockSpec((tm, d), lambda i: (i, 0)),
                  pl.BlockSpec((d, d_ff), lambda i: (0, 0), pipeline_mode=pl.Buffered(1)),
                  pl.BlockSpec((d_ff, d), lambda i: (0, 0), pipeline_mode=pl.Buffered(1)),
                  pl.BlockSpec((1, d), lambda i: (0, 0)),
                  pl.BlockSpec((1, d), lambda i: (0, 0))],
        out_specs=pl.BlockSpec((tm, d), lambda i: (i, 0)),
        out_shape=jax.ShapeDtypeStruct((t, d), _F32),
        compiler_params=pltpu.CompilerParams(dimension_semantics=("arbitrary",),
                                             vmem_limit_bytes=VMEM_LIMIT_BYTES),
        name="mlp_ln",
    )(h2d, w_up_bf16, w_down_bf16, g.reshape(1, d), b.reshape(1, d))


def _bias_kernel(tab_ref, bias_ref, cfar_ref):
    h = pl.program_id(0)
    w = MOBA_BLOCK
    key = lax.broadcasted_iota(jnp.int32, (w, w), 0)
    qry = lax.broadcasted_iota(jnp.int32, (w, w), 1)
    far = tab_ref[REL_BUCKETS - 1, h] * LOG2E
    for d in range(2):
        dist = qry - key + d * w
        val = jnp.full((w, w), far, _F32)
        for bkt in range(REL_BUCKETS - 2, -1, -1):
            val = jnp.where(dist <= _BUCKET_UB[bkt], tab_ref[bkt, h] * LOG2E, val)
        if d == 0:
            val = jnp.where(dist >= 0, val, _NEG_INF)
        bias_ref[0, d] = val
    cfar_ref[0] = jnp.full((8, w), far, _F32)


def _bias_tiles(rel_bias):
    n_heads = rel_bias.shape[1]
    w = MOBA_BLOCK
    return pl.pallas_call(
        _bias_kernel,
        grid=(n_heads,),
        in_specs=[pl.BlockSpec(memory_space=pltpu.SMEM)],
        out_specs=[pl.BlockSpec((1, 2, w, w), lambda h: (h, 0, 0, 0)),
                   pl.BlockSpec((1, 8, w), lambda h: (h, 0, 0))],
        out_shape=[jax.ShapeDtypeStruct((n_heads, 2, w, w), _F32),
                   jax.ShapeDtypeStruct((n_heads, 8, w), _F32)],
        compiler_params=pltpu.CompilerParams(dimension_semantics=("arbitrary",)),
        name="t5_bias_tiles",
    )(rel_bias.astype(_F32))


def _head_rows(hh):
    row = lax.broadcasted_iota(jnp.int32, (LANES, MOBA_BLOCK), 0)
    return (row // HEAD_DIM) == hh


def _store_transposed_operands(q_ref, v_ref, qT_ref, vT_ref, hh, nb):
    w = MOBA_BLOCK
    mine = _head_rows(hh)
    for i in range(nb):
        q_t = q_ref[0, i * w:(i + 1) * w, :].astype(_F32).T
        qT_ref[i] = jnp.where(mine, q_t, 0.0).astype(_BF16)

    @pl.when(hh == 0)
    def _():
        for i in range(nb):
            vT_ref[i] = v_ref[0, i * w:(i + 1) * w, :].astype(_F32).T.astype(_BF16)


def _merge_and_store(o_ref, oT_ref, res, hh, nb):
    w = MOBA_BLOCK

    @pl.when(hh == 0)
    def _():
        for i in range(nb):
            oT_ref[i] = res[i]

    @pl.when(hh == HEADS_PER_STEP - 1)
    def _():
        mine = _head_rows(hh)
        for i in range(nb):
            both = jnp.where(mine, res[i], oT_ref[i])
            o_ref[0, i * w:(i + 1) * w, :] = both.T.astype(_BF16)


def _causal_tiles(nb):
    return [(i, j) for i in range(nb) for j in range(i, -1, -1)]


def _software_pipeline(n_items, stages, lookahead):
    n_stages = len(stages)
    for step in range(n_items + (n_stages - 1) * lookahead):
        for s, stage in enumerate(stages):
            t = step - s * lookahead
            if 0 <= t < n_items:
                stage(t)


def _moba_kernel(q_ref, k_ref, v_ref, bias_ref, cfar_ref, o_ref,
                 qT_ref, vT_ref, kparts_ref, oT_ref, *, nb, topk):
    w = MOBA_BLOCK
    hh = pl.program_id(2)
    _store_transposed_operands(q_ref, v_ref, qT_ref, vT_ref, hh, nb)

    @pl.when(hh == 0)
    def _():
        rows = []
        for j in range(nb):
            kj = k_ref[0, j * w:(j + 1) * w, :].astype(_F32)
            rows.append(jnp.mean(kj, axis=0, keepdims=True))
        kmean = jnp.concatenate(rows + [jnp.zeros((16 - nb, LANES), _F32)], axis=0)
        p0 = kmean.astype(_BF16)
        r1 = kmean - p0.astype(_F32)
        p1 = r1.astype(_BF16)
        p2 = (r1 - p1.astype(_F32)).astype(_BF16)
        kparts_ref[0:16, :] = p0
        kparts_ref[16:32, :] = p1
        kparts_ref[32:48, :] = p2

    jidx = lax.broadcasted_iota(jnp.int32, (8, w), 0)
    cfar = cfar_ref[0][0:1, :]
    addrows = [None]
    for i in range(1, nb):
        g3 = jnp.dot(kparts_ref[...], qT_ref[i], preferred_element_type=_F32)
        gate = g3[0:8] + g3[16:24] + g3[32:40]
        cnt = jnp.zeros((8, w), jnp.int32)
        for jp in range(i):
            rowv = gate[jp:jp + 1, :]
            beats = (rowv > gate) | ((rowv == gate) & (jp < jidx))
            cnt = cnt + jnp.where(beats, 1, 0)
        sel = (cnt < topk) & (jidx < i)
        addrows.append(jnp.where(sel, jnp.where(jidx < i - 1, cfar, 0.0), _NEG_INF))

    tiles = _causal_tiles(nb)
    scores = {}
    state = {}
    res = [None] * nb

    def issue_scores(t):
        i, j = tiles[t]
        s = jnp.dot(k_ref[0, j * w:(j + 1) * w, :], qT_ref[i], preferred_element_type=_F32)
        if j == i:
            s = s + bias_ref[0, 0]
        elif j == i - 1:
            s = s + bias_ref[0, 1] + addrows[i][j:j + 1, :]
        else:
            s = s + addrows[i][j:j + 1, :]
        scores[t] = s

    def softmax_accumulate(t):
        i, j = tiles[t]
        s = scores.pop(t)
        mt = jnp.max(s, axis=0, keepdims=True)
        if j == i:
            m = mt
            p = jnp.exp2(s - m)
            l = jnp.sum(p, axis=0, keepdims=True)
            acc = jnp.dot(vT_ref[j], p.astype(_BF16), preferred_element_type=_F32)
        else:
            m_old, l_old, acc_old = state["mla"]
            m = jnp.maximum(m_old, mt)
            a = jnp.exp2(m_old - m)
            p = jnp.exp2(s - m)
            l = a * l_old + jnp.sum(p, axis=0, keepdims=True)
            acc = a * acc_old + jnp.dot(vT_ref[j], p.astype(_BF16), preferred_element_type=_F32)
        state["mla"] = (m, l, acc)
        if j == 0:
            res[i] = acc * (1.0 / l)

    _software_pipeline(len(tiles), [issue_scores, softmax_accumulate], MOBA_LOOKAHEAD)
    _merge_and_store(o_ref, oT_ref, res, hh, nb)


def _attention_call(kernel, qkv3, extra_inputs, extra_specs, scratch, name):
    b, s, d3 = qkv3.shape
    d = d3 // 3
    w = MOBA_BLOCK
    assert s % w == 0 and d % LANES == 0
    nb = s // w
    assert nb <= 8
    n_lane_blocks = d // LANES
    qkv_specs = [pl.BlockSpec((1, s, LANES), lambda hp, bi, hh, off=off: (bi, 0, off + hp))
                 for off in (0, n_lane_blocks, 2 * n_lane_blocks)]
    return pl.pallas_call(
        functools.partial(kernel, nb=nb),
        grid=(n_lane_blocks, b, HEADS_PER_STEP),
        in_specs=qkv_specs + extra_specs,
        out_specs=pl.BlockSpec((1, s, LANES), lambda hp, bi, hh: (bi, 0, hp)),
        out_shape=jax.ShapeDtypeStruct((b, s, d), _BF16),
        scratch_shapes=[pltpu.VMEM((nb, LANES, w), _BF16),
                        pltpu.VMEM((nb, LANES, w), _BF16)]
                       + scratch
                       + [pltpu.VMEM((nb, LANES, w), _F32)],
        compiler_params=pltpu.CompilerParams(
            dimension_semantics=("arbitrary", "arbitrary", "arbitrary"),
            vmem_limit_bytes=VMEM_LIMIT_BYTES),
        name=name,
    )(qkv3, qkv3, qkv3, *extra_inputs)


def _moba_attention(qkv3, bias_tiles, cfar):
    w = MOBA_BLOCK
    nb = qkv3.shape[1] // w
    topk = min(MOBA_TOPK, nb)
    extra_specs = [
        pl.BlockSpec((1, 2, w, w), lambda hp, bi, hh: (hp * HEADS_PER_STEP + hh, 0, 0, 0)),
        pl.BlockSpec((1, 8, w), lambda hp, bi, hh: (hp * HEADS_PER_STEP + hh, 0, 0)),
    ]
    return _attention_call(functools.partial(_moba_kernel, topk=topk), qkv3,
                           [bias_tiles, cfar], extra_specs,
                           [pltpu.VMEM((48, LANES), _BF16)], "moba_attention")


def _sb_kernel(q_ref, k_ref, v_ref, o_ref, qT_ref, vT_ref, oT_ref, *, nb):
    w = MOBA_BLOCK
    hh = pl.program_id(2)
    _store_transposed_operands(q_ref, v_ref, qT_ref, vT_ref, hh, nb)

    key = lax.broadcasted_iota(jnp.int32, (w, w), 0)
    col = lax.broadcasted_iota(jnp.int32, (w, w), 1)
    causal = key < col
    upper = jnp.where(col > key, 1.0, 0.0).astype(_BF16)

    tiles = _causal_tiles(nb)
    logits, partial, state = {}, {}, {}
    res = [None] * nb

    def issue_logits(t):
        i, j = tiles[t]
        logits[t] = jnp.dot(k_ref[0, j * w:(j + 1) * w, :], qT_ref[i],
                            preferred_element_type=_F32)

    def issue_cumsum(t):
        i, j = tiles[t]
        z = logits.pop(t)
        neg_abs = pltpu.bitcast(pltpu.bitcast(z, jnp.uint32) | jnp.uint32(0x80000000), _F32)
        sp = jnp.maximum(z, 0.0) + jnp.log(1.0 + jnp.exp2(neg_abs)) * LOG2E
        if j == i:
            sp = jnp.where(causal, sp, 0.0)
        hi = sp.astype(_BF16)
        lo = (sp - hi.astype(_F32)).astype(_BF16)
        after = (jnp.dot(upper, hi, preferred_element_type=_F32)
                 + jnp.dot(upper, lo, preferred_element_type=_F32))
        partial[t] = (z - sp, after, sp[0:1, :])

    def weigh_values(t):
        i, j = tiles[t]
        log_sig, after, sp_first = partial.pop(t)
        a = jnp.exp2(log_sig - after)
        if j == i:
            a = jnp.where(causal, a, 0.0)
        pv = jnp.dot(vT_ref[j], a.astype(_BF16), preferred_element_type=_F32)
        tile_total = after[0:1, :] + sp_first
        if j == i:
            acc, r = pv, tile_total
        else:
            acc_old, r_old = state["acc_r"]
            acc = acc_old + pv * jnp.exp2(-r_old)
            r = r_old + tile_total
        state["acc_r"] = (acc, r)
        if j == 0:
            res[i] = acc

    _software_pipeline(len(tiles), [issue_logits, issue_cumsum, weigh_values], SB_LOOKAHEAD)
    _merge_and_store(o_ref, oT_ref, res, hh, nb)


def _sb_attention(qkv3):
    return _attention_call(_sb_kernel, qkv3, [], [], [], "stick_breaking_attention")


def kernel(x, rel_bias, w_qkv, w_o, ln_mix_g, ln_mix_b, w_up, w_down, ln_ffn_g, ln_ffn_b):
    b, s, d = x.shape
    depth = w_qkv.shape[0]
    alpha = (2.0 * depth) ** 0.25
    assert d % (HEADS_PER_STEP * HEAD_DIM) == 0
    bias_tiles, cfar = _bias_tiles(rel_bias)
    h = x.reshape(b * s, d)
    for i in range(depth):
        qkv = _qkv_proj(h, w_qkv[i].astype(_BF16)).reshape(b, s, 3 * d)
        if i % 2 == 0:
            o = _moba_attention(qkv, bias_tiles, cfar)
        else:
            o = _sb_attention(qkv)
        h = _oproj_ln(o.reshape(b * s, d), h, w_o[i].astype(_BF16),
                      ln_mix_g[i], ln_mix_b[i], alpha)
        h = _mlp_ln(h, w_up[i].astype(_BF16), w_down[i].astype(_BF16),
                    ln_ffn_g[i], ln_ffn_b[i], alpha)
    return h.reshape(b, s, d)
```

```python
import functools
import math

import numpy as np
import jax
import jax.numpy as jnp
from jax import lax
from jax.experimental import pallas as pl
from jax.experimental.pallas import tpu as pltpu

HEAD_DIM = 64
HEADS_PER_STEP = 2
LANES = 128
MOBA_BLOCK = 256
MOBA_TOPK = 3
REL_BUCKETS = 32
REL_MAX_DIST = 128
LN_EPS = 1e-5
TOKEN_TILE = 512
FF_CHUNK = 1024
VMEM_LIMIT_BYTES = 48 * 1024 * 1024
MOBA_LOOKAHEAD = 4
SB_LOOKAHEAD = 2
SB_NEAR_TILES = 2
SB_SKIP_LOG2 = 152.0
LOG2E = math.log2(math.e)

_F32 = jnp.float32
_BF16 = jnp.bfloat16
_NEG_INF = float("-inf")


def _t5_bucket_upper_bounds():
    n = np.arange(0, 4 * REL_MAX_DIST)
    max_exact = REL_BUCKETS // 2
    nf = np.maximum(n, 1).astype(np.float64)
    large = max_exact + (np.log(nf / max_exact) / math.log(REL_MAX_DIST / max_exact)
                         * (REL_BUCKETS - max_exact)).astype(np.int64)
    large = np.minimum(large, REL_BUCKETS - 1)
    bucket = np.where(n < max_exact, n, large)
    assert bucket[-1] == REL_BUCKETS - 1 and np.all(np.diff(bucket) >= 0)
    return tuple(int(n[bucket == b].max()) for b in range(REL_BUCKETS - 1))


_BUCKET_UB = _t5_bucket_upper_bounds()
assert _BUCKET_UB[-1] < MOBA_BLOCK


def _layer_norm(y, g, b):
    mu = jnp.mean(y, axis=-1, keepdims=True)
    yc = y - mu
    var = jnp.mean(yc * yc, axis=-1, keepdims=True)
    return yc * lax.rsqrt(var + LN_EPS) * g + b


def _qkv_kernel(x_ref, w_ref, o_ref, *, d_model, q_scale):
    xb = x_ref[...].astype(_BF16)
    for c in range(3):
        cols = slice(c * d_model, (c + 1) * d_model)
        acc = jnp.dot(xb, w_ref[:, cols], preferred_element_type=_F32)
        if c == 0:
            acc = acc * q_scale
        o_ref[:, cols] = acc.astype(_BF16)


def _qkv_proj(h2d, w_qkv_bf16):
    t, d = h2d.shape
    tm = min(TOKEN_TILE, t)
    assert t % tm == 0
    return pl.pallas_call(
        functools.partial(_qkv_kernel, d_model=d, q_scale=LOG2E * HEAD_DIM ** -0.5),
        grid=(t // tm,),
        in_specs=[pl.BlockSpec((tm, d), lambda i: (i, 0)),
                  pl.BlockSpec((d, 3 * d), lambda i: (0, 0), pipeline_mode=pl.Buffered(1))],
        out_specs=pl.BlockSpec((tm, 3 * d), lambda i: (i, 0)),
        out_shape=jax.ShapeDtypeStruct((t, 3 * d), _BF16),
        compiler_params=pltpu.CompilerParams(dimension_semantics=("arbitrary",),
                                             vmem_limit_bytes=VMEM_LIMIT_BYTES),
        name="qkv_proj",
    )(h2d, w_qkv_bf16)


def _oproj_ln_kernel(o_ref, h_ref, w_ref, g_ref, b_ref, out_ref, *, alpha):
    acc = jnp.dot(o_ref[...], w_ref[...], preferred_element_type=_F32)
    out_ref[...] = _layer_norm(alpha * h_ref[...] + acc, g_ref[...], b_ref[...])


def _oproj_ln(o2d, h2d, w_o_bf16, g, b, alpha):
    t, d = h2d.shape
    tm = min(TOKEN_TILE, t)
    assert t % tm == 0
    return pl.pallas_call(
        functools.partial(_oproj_ln_kernel, alpha=alpha),
        grid=(t // tm,),
        in_specs=[pl.BlockSpec((tm, d), lambda i: (i, 0)),
                  pl.BlockSpec((tm, d), lambda i: (i, 0)),
                  pl.BlockSpec((d, d), lambda i: (0, 0), pipeline_mode=pl.Buffered(1)),
                  pl.BlockSpec((1, d), lambda i: (0, 0)),
                  pl.BlockSpec((1, d), lambda i: (0, 0))],
        out_specs=pl.BlockSpec((tm, d), lambda i: (i, 0)),
        out_shape=jax.ShapeDtypeStruct((t, d), _F32),
        compiler_params=pltpu.CompilerParams(dimension_semantics=("arbitrary",),
                                             vmem_limit_bytes=VMEM_LIMIT_BYTES),
        name="oproj_ln",
    )(o2d, h2d, w_o_bf16, g.reshape(1, d), b.reshape(1, d))


def _mlp_ln_kernel(h_ref, wu_ref, wd_ref, g_ref, b_ref, out_ref, *, alpha, d_ff, fc):
    h = h_ref[...]
    hb = h.astype(_BF16)
    acc = alpha * h
    for c in range(d_ff // fc):
        u = jnp.dot(hb, wu_ref[:, c * fc:(c + 1) * fc], preferred_element_type=_F32)
        u = jnp.square(jnp.maximum(u, 0.0)).astype(_BF16)
        acc = acc + jnp.dot(u, wd_ref[c * fc:(c + 1) * fc, :], preferred_element_type=_F32)
    out_ref[...] = _layer_norm(acc, g_ref[...], b_ref[...])


def _mlp_ln(h2d, w_up_bf16, w_down_bf16, g, b, alpha):
    t, d = h2d.shape
    d_ff = w_up_bf16.shape[1]
    tm = min(TOKEN_TILE, t)
    fc = min(FF_CHUNK, d_ff)
    assert t % tm == 0 and d_ff % fc == 0
    return pl.pallas_call(
        functools.partial(_mlp_ln_kernel, alpha=alpha, d_ff=d_ff, fc=fc),
        grid=(t // tm,),
        in_specs=[pl.BlockSpec((tm, d), lambda i: (i, 0)),
                  pl.BlockSpec((d, d_ff), lambda i: (0, 0), pipeline_mode=pl.Buffered(1)),
                  pl.BlockSpec((d_ff, d), lambda i: (0, 0), pipeline_mode=pl.Buffered(1)),
                  pl.BlockSpec((1, d), lambda i: (0, 0)),
                  pl.BlockSpec((1, d), lambda i: (0, 0))],
        out_specs=pl.BlockSpec((tm, d), lambda i: (i, 0)),
        out_shape=jax.ShapeDtypeStruct((t, d), _F32),
        compiler_params=pltpu.CompilerParams(dimension_semantics=("arbitrary",),
                                             vmem_limit_bytes=VMEM_LIMIT_BYTES),
        name="mlp_ln",
    )(h2d, w_up_bf16, w_down_bf16, g.reshape(1, d), b.reshape(1, d))


def _bias_kernel(tab_ref, bias_ref, cfar_ref):
    h = pl.program_id(0)
    w = MOBA_BLOCK
    key = lax.broadcasted_iota(jnp.int32, (w, w), 0)
    qry = lax.broadcasted_iota(jnp.int32, (w, w), 1)
    far = tab_ref[REL_BUCKETS - 1, h] * LOG2E
    for d in range(2):
        dist = qry - key + d * w
        val = jnp.full((w, w), far, _F32)
        for bkt in range(REL_BUCKETS - 2, -1, -1):
            val = jnp.where(dist <= _BUCKET_UB[bkt], tab_ref[bkt, h] * LOG2E, val)
        if d == 0:
            val = jnp.where(dist >= 0, val, _NEG_INF)
        bias_ref[0, d] = val
    cfar_ref[0] = jnp.full((8, w), far, _F32)


def _bias_tiles(rel_bias):
    n_heads = rel_bias.shape[1]
    w = MOBA_BLOCK
    return pl.pallas_call(
        _bias_kernel,
        grid=(n_heads,),
        in_specs=[pl.BlockSpec(memory_space=pltpu.SMEM)],
        out_specs=[pl.BlockSpec((1, 2, w, w), lambda h: (h, 0, 0, 0)),
                   pl.BlockSpec((1, 8, w), lambda h: (h, 0, 0))],
        out_shape=[jax.ShapeDtypeStruct((n_heads, 2, w, w), _F32),
                   jax.ShapeDtypeStruct((n_heads, 8, w), _F32)],
        compiler_params=pltpu.CompilerParams(dimension_semantics=("arbitrary",)),
        name="t5_bias_tiles",
    )(rel_bias.astype(_F32))


def _head_rows(hh):
    row = lax.broadcasted_iota(jnp.int32, (LANES, MOBA_BLOCK), 0)
    return (row // HEAD_DIM) == hh


def _store_transposed_operands(q_ref, v_ref, qT_ref, vT_ref, hh, nb):
    w = MOBA_BLOCK
    mine = _head_rows(hh)
    for i in range(nb):
        q_t = q_ref[0, i * w:(i + 1) * w, :].astype(_F32).T
        qT_ref[i] = jnp.where(mine, q_t, 0.0).astype(_BF16)

    @pl.when(hh == 0)
    def _():
        for i in range(nb):
            vT_ref[i] = v_ref[0, i * w:(i + 1) * w, :].astype(_F32).T.astype(_BF16)


def _merge_and_store(o_ref, oT_ref, hh, nb):
    w = MOBA_BLOCK

    @pl.when(hh == HEADS_PER_STEP - 1)
    def _():
        first = _head_rows(0)
        for i in range(nb):
            both = jnp.where(first, oT_ref[0, i], oT_ref[1, i])
            o_ref[0, i * w:(i + 1) * w, :] = both.T.astype(_BF16)


def _causal_tiles(nb, newest=None):
    tiles = []
    for i in range(nb):
        keys = list(range(i, -1, -1))
        if newest is not None:
            keys = keys[:newest] if newest > 0 else keys[-newest:]
        tiles += [(i, j) for j in keys]
    return tiles


def _software_pipeline(n_items, stages, lookahead):
    n_stages = len(stages)
    for step in range(n_items + (n_stages - 1) * lookahead):
        for s, stage in enumerate(stages):
            t = step - s * lookahead
            if 0 <= t < n_items:
                stage(t)


def _moba_kernel(q_ref, k_ref, v_ref, bias_ref, cfar_ref, o_ref,
                 qT_ref, vT_ref, kparts_ref, vTh_ref, oT_ref, *, nb, topk):
    w = MOBA_BLOCK
    hh = pl.program_id(2)
    _store_transposed_operands(q_ref, v_ref, qT_ref, vT_ref, hh, nb)

    mine = _head_rows(hh)
    row = lax.broadcasted_iota(jnp.int32, (LANES, w), 0)
    ones_row = jnp.where(row == (1 - hh) * HEAD_DIM, 1.0, 0.0).astype(_BF16)
    for i in range(nb):
        vTh_ref[i] = jnp.where(mine, vT_ref[i], ones_row)

    @pl.when(hh == 0)
    def _():
        rows = []
        for j in range(nb):
            kj = k_ref[0, j * w:(j + 1) * w, :].astype(_F32)
            rows.append(jnp.mean(kj, axis=0, keepdims=True))
        kmean = jnp.concatenate(rows + [jnp.zeros((16 - nb, LANES), _F32)], axis=0)
        p0 = kmean.astype(_BF16)
        r1 = kmean - p0.astype(_F32)
        p1 = r1.astype(_BF16)
        p2 = (r1 - p1.astype(_F32)).astype(_BF16)
        kparts_ref[0:16, :] = p0
        kparts_ref[16:32, :] = p1
        kparts_ref[32:48, :] = p2

    jidx = lax.broadcasted_iota(jnp.int32, (8, w), 0)
    cfar = cfar_ref[0][0:1, :]
    addrows = [None]
    for i in range(1, nb):
        g3 = jnp.dot(kparts_ref[...], qT_ref[i], preferred_element_type=_F32)
        gate = g3[0:8] + g3[16:24] + g3[32:40]
        cnt = jnp.zeros((8, w), jnp.int32)
        for jp in range(i):
            rowv = gate[jp:jp + 1, :]
            beats = (rowv > gate) | ((rowv == gate) & (jp < jidx))
            cnt = cnt + jnp.where(beats, 1, 0)
        sel = (cnt < topk) & (jidx < i)
        addrows.append(jnp.where(sel, jnp.where(jidx < i - 1, cfar, 0.0), _NEG_INF))

    tiles = _causal_tiles(nb)
    scores = {}
    state = {}

    def issue_scores(t):
        i, j = tiles[t]
        s = jnp.dot(k_ref[0, j * w:(j + 1) * w, :], qT_ref[i], preferred_element_type=_F32)
        if j >= i - 1:
            s = s + bias_ref[0, i - j]
        scores[t] = s

    def softmax_accumulate(t):
        i, j = tiles[t]
        s = scores.pop(t)
        mt = jnp.max(s, axis=0, keepdims=True)
        if j == i:
            m = mt
            p = jnp.exp2(s - m)
            acc = jnp.dot(vTh_ref[j], p.astype(_BF16), preferred_element_type=_F32)
        else:
            arow = addrows[i][j:j + 1, :]
            m_old, acc_old = state["m_acc"]
            m = jnp.maximum(m_old, mt + arow)
            a = jnp.exp2(m_old - m)
            p = jnp.exp2(s - (m - arow))
            acc = a * acc_old + jnp.dot(vTh_ref[j], p.astype(_BF16), preferred_element_type=_F32)
        state["m_acc"] = (m, acc)
        if j == 0:
            l = jnp.where(hh == 0, acc[HEAD_DIM:HEAD_DIM + 1, :], acc[0:1, :])
            oT_ref[hh, i] = acc * (1.0 / l)

    _software_pipeline(len(tiles), [issue_scores, softmax_accumulate], MOBA_LOOKAHEAD)
    _merge_and_store(o_ref, oT_ref, hh, nb)


def _attention_call(kernel, qkv3, extra_inputs, extra_specs, scratch, name):
    b, s, d3 = qkv3.shape
    d = d3 // 3
    w = MOBA_BLOCK
    assert s % w == 0 and d % LANES == 0
    nb = s // w
    assert nb <= 8
    n_lane_blocks = d // LANES
    qkv_specs = [pl.BlockSpec((1, s, LANES), lambda hp, bi, hh, off=off: (bi, 0, off + hp))
                 for off in (0, n_lane_blocks, 2 * n_lane_blocks)]
    return pl.pallas_call(
        functools.partial(kernel, nb=nb),
        grid=(n_lane_blocks, b, HEADS_PER_STEP),
        in_specs=qkv_specs + extra_specs,
        out_specs=pl.BlockSpec((1, s, LANES), lambda hp, bi, hh: (bi, 0, hp)),
        out_shape=jax.ShapeDtypeStruct((b, s, d), _BF16),
        scratch_shapes=[pltpu.VMEM((nb, LANES, w), _BF16),
                        pltpu.VMEM((nb, LANES, w), _BF16)]
                       + scratch
                       + [pltpu.VMEM((HEADS_PER_STEP, nb, LANES, w), _F32)],
        compiler_params=pltpu.CompilerParams(
            dimension_semantics=("arbitrary", "arbitrary", "arbitrary"),
            vmem_limit_bytes=VMEM_LIMIT_BYTES),
        name=name,
    )(qkv3, qkv3, qkv3, *extra_inputs)


def _moba_attention(qkv3, bias_tiles, cfar):
    w = MOBA_BLOCK
    nb = qkv3.shape[1] // w
    topk = min(MOBA_TOPK, nb)
    extra_specs = [
        pl.BlockSpec((1, 2, w, w), lambda hp, bi, hh: (hp * HEADS_PER_STEP + hh, 0, 0, 0)),
        pl.BlockSpec((1, 8, w), lambda hp, bi, hh: (hp * HEADS_PER_STEP + hh, 0, 0)),
    ]
    return _attention_call(functools.partial(_moba_kernel, topk=topk), qkv3,
                           [bias_tiles, cfar], extra_specs,
                           [pltpu.VMEM((48, LANES), _BF16),
                            pltpu.VMEM((nb, LANES, w), _BF16)],
                           "moba_attention")


def _sb_kernel(q_ref, k_ref, v_ref, o_ref, qT_ref, vT_ref, r_ref, oT_ref, *, nb):
    w = MOBA_BLOCK
    hh = pl.program_id(2)
    _store_transposed_operands(q_ref, v_ref, qT_ref, vT_ref, hh, nb)

    key = lax.broadcasted_iota(jnp.int32, (w, w), 0)
    col = lax.broadcasted_iota(jnp.int32, (w, w), 1)
    causal = key < col
    upper = jnp.where(col > key, 1.0, 0.0).astype(_BF16)

    def run_tiles(tiles, resume):
        logits, partial, state, r_final = {}, {}, {}, {}
        first_key = {}
        for i, j in tiles:
            first_key.setdefault(i, j)
        last_key = {i: j for i, j in tiles}

        def issue_logits(t):
            i, j = tiles[t]
            logits[t] = jnp.dot(k_ref[0, j * w:(j + 1) * w, :], qT_ref[i],
                                preferred_element_type=_F32)

        def issue_cumsum(t):
            i, j = tiles[t]
            z = logits.pop(t)
            neg_abs = pltpu.bitcast(pltpu.bitcast(z, jnp.uint32) | jnp.uint32(0x80000000), _F32)
            sp = jnp.maximum(z, 0.0) + jnp.log(1.0 + jnp.exp2(neg_abs)) * LOG2E
            if j == i:
                sp = jnp.where(causal, sp, 0.0)
            hi = sp.astype(_BF16)
            lo = (sp - hi.astype(_F32)).astype(_BF16)
            after = (jnp.dot(upper, hi, preferred_element_type=_F32)
                     + jnp.dot(upper, lo, preferred_element_type=_F32))
            partial[t] = (z - sp, after, sp[0:1, :])

        def weigh_values(t):
            i, j = tiles[t]
            log_sig, after, sp_first = partial.pop(t)
            a = jnp.exp2(log_sig - after)
            if j == i:
                a = jnp.where(causal, a, 0.0)
            pv = jnp.dot(vT_ref[j], a.astype(_BF16), preferred_element_type=_F32)
            tile_total = after[0:1, :] + sp_first
            if j == i:
                acc, r = pv, tile_total
            else:
                if j == first_key[i]:
                    acc_old, r_old = oT_ref[hh, i], r_ref[i]
                else:
                    acc_old, r_old = state["acc_r"]
                acc = acc_old + pv * jnp.exp2(-r_old)
                r = r_old + tile_total
            state["acc_r"] = (acc, r)
            if j == last_key[i]:
                oT_ref[hh, i] = acc
                r_final[i] = r
                if not resume:
                    r_ref[i] = r

        _software_pipeline(len(tiles), [issue_logits, issue_cumsum, weigh_values], SB_LOOKAHEAD)
        return r_final

    r_near = run_tiles(_causal_tiles(nb, newest=SB_NEAR_TILES), resume=False)
    far_tiles = _causal_tiles(nb, newest=-SB_NEAR_TILES)
    if far_tiles:
        r_min = None
        for i in sorted({i for i, _ in far_tiles}):
            r_min = r_near[i] if r_min is None else jnp.minimum(r_min, r_near[i])

        @pl.when(jnp.min(r_min) <= SB_SKIP_LOG2)
        def _():
            run_tiles(far_tiles, resume=True)

    _merge_and_store(o_ref, oT_ref, hh, nb)


def _sb_attention(qkv3):
    nb = qkv3.shape[1] // MOBA_BLOCK
    return _attention_call(_sb_kernel, qkv3, [], [],
                           [pltpu.VMEM((nb, 1, MOBA_BLOCK), _F32)],
                           "stick_breaking_attention")


def kernel(x, rel_bias, w_qkv, w_o, ln_mix_g, ln_mix_b, w_up, w_down, ln_ffn_g, ln_ffn_b):
    b, s, d = x.shape
    depth = w_qkv.shape[0]
    alpha = (2.0 * depth) ** 0.25
    assert d % (HEADS_PER_STEP * HEAD_DIM) == 0
    bias_tiles, cfar = _bias_tiles(rel_bias)
    h = x.reshape(b * s, d)
    for i in range(depth):
        qkv = _qkv_proj(h, w_qkv[i].astype(_BF16)).reshape(b, s, 3 * d)
        if i % 2 == 0:
            o = _moba_attention(qkv, bias_tiles, cfar)
        else:
            o = _sb_attention(qkv)
        h = _oproj_ln(o.reshape(b * s, d), h, w_o[i].astype(_BF16),
                      ln_mix_g[i], ln_mix_b[i], alpha)
        h = _mlp_ln(h, w_up[i].astype(_BF16), w_down[i].astype(_BF16),
                    ln_ffn_g[i], ln_ffn_b[i], alpha)
    return h.reshape(b, s, d)
```

```python
import functools
import math

import numpy as np
import jax
import jax.numpy as jnp
from jax import lax
from jax.experimental import pallas as pl
from jax.experimental.pallas import tpu as pltpu

HEAD_DIM = 64
HEADS_PER_STEP = 2
LANES = 128
MOBA_BLOCK = 256
MOBA_TOPK = 3
REL_BUCKETS = 32
REL_MAX_DIST = 128
LN_EPS = 1e-5
TOKEN_TILE = 512
FF_CHUNK = 1024
VMEM_LIMIT_BYTES = 48 * 1024 * 1024
MOBA_LOOKAHEAD = 4
SB_LOOKAHEAD = 2
SB_NEAR_TILES = 2
SB_SKIP_LOG2 = 152.0
LOG2E = math.log2(math.e)

_F32 = jnp.float32
_BF16 = jnp.bfloat16
_NEG_INF = float("-inf")


def _t5_bucket_upper_bounds():
    n = np.arange(0, 4 * REL_MAX_DIST)
    max_exact = REL_BUCKETS // 2
    nf = np.maximum(n, 1).astype(np.float64)
    large = max_exact + (np.log(nf / max_exact) / math.log(REL_MAX_DIST / max_exact)
                         * (REL_BUCKETS - max_exact)).astype(np.int64)
    large = np.minimum(large, REL_BUCKETS - 1)
    bucket = np.where(n < max_exact, n, large)
    assert bucket[-1] == REL_BUCKETS - 1 and np.all(np.diff(bucket) >= 0)
    return tuple(int(n[bucket == b].max()) for b in range(REL_BUCKETS - 1))


_BUCKET_UB = _t5_bucket_upper_bounds()
assert _BUCKET_UB[-1] < MOBA_BLOCK


def _layer_norm(y, g, b):
    mu = jnp.mean(y, axis=-1, keepdims=True)
    yc = y - mu
    var = jnp.mean(yc * yc, axis=-1, keepdims=True)
    return yc * lax.rsqrt(var + LN_EPS) * g + b


_CONTRACT_LAST = (((1,), (1,)), ((), ()))


def _qkv_kernel(x_ref, wqT_ref, wk_ref, wvT_ref, qT_ref, k_ref, vT_ref, *, q_scale):
    xb = x_ref[...].astype(_BF16)
    q_t = lax.dot_general(wqT_ref[...], xb, _CONTRACT_LAST, preferred_element_type=_F32)
    qT_ref[...] = (q_t * q_scale).astype(_BF16)
    k_ref[...] = jnp.dot(xb, wk_ref[...], preferred_element_type=_F32).astype(_BF16)
    v_t = lax.dot_general(wvT_ref[...], xb, _CONTRACT_LAST, preferred_element_type=_F32)
    vT_ref[...] = v_t.astype(_BF16)


def _qkv_proj(h2d, w_qkv):
    t, d = h2d.shape
    tm = min(TOKEN_TILE, t)
    assert t % tm == 0 and w_qkv.shape == (d, 3 * d)
    wq_t = w_qkv[:, :d].T.astype(_BF16)
    wk = w_qkv[:, d:2 * d].astype(_BF16)
    wv_t = w_qkv[:, 2 * d:].T.astype(_BF16)
    weight_spec = pl.BlockSpec((d, d), lambda i: (0, 0), pipeline_mode=pl.Buffered(1))
    transposed = jax.ShapeDtypeStruct((d, t), _BF16)
    return pl.pallas_call(
        functools.partial(_qkv_kernel, q_scale=LOG2E * HEAD_DIM ** -0.5),
        grid=(t // tm,),
        in_specs=[pl.BlockSpec((tm, d), lambda i: (i, 0)), weight_spec, weight_spec, weight_spec],
        out_specs=[pl.BlockSpec((d, tm), lambda i: (0, i)),
                   pl.BlockSpec((tm, d), lambda i: (i, 0)),
                   pl.BlockSpec((d, tm), lambda i: (0, i))],
        out_shape=[transposed, jax.ShapeDtypeStruct((t, d), _BF16), transposed],
        compiler_params=pltpu.CompilerParams(dimension_semantics=("arbitrary",),
                                             vmem_limit_bytes=VMEM_LIMIT_BYTES),
        name="qkv_proj",
    )(h2d, wq_t, wk, wv_t)


def _oproj_ln_kernel(o_ref, h_ref, w_ref, g_ref, b_ref, out_ref, *, alpha):
    acc = jnp.dot(o_ref[...], w_ref[...], preferred_element_type=_F32)
    out_ref[...] = _layer_norm(alpha * h_ref[...] + acc, g_ref[...], b_ref[...])


def _oproj_ln(o2d, h2d, w_o_bf16, g, b, alpha):
    t, d = h2d.shape
    tm = min(TOKEN_TILE, t)
    assert t % tm == 0
    return pl.pallas_call(
        functools.partial(_oproj_ln_kernel, alpha=alpha),
        grid=(t // tm,),
        in_specs=[pl.BlockSpec((tm, d), lambda i: (i, 0)),
                  pl.BlockSpec((tm, d), lambda i: (i, 0)),
                  pl.BlockSpec((d, d), lambda i: (0, 0), pipeline_mode=pl.Buffered(1)),
                  pl.BlockSpec((1, d), lambda i: (0, 0)),
                  pl.BlockSpec((1, d), lambda i: (0, 0))],
        out_specs=pl.BlockSpec((tm, d), lambda i: (i, 0)),
        out_shape=jax.ShapeDtypeStruct((t, d), _F32),
        compiler_params=pltpu.CompilerParams(dimension_semantics=("arbitrary",),
                                             vmem_limit_bytes=VMEM_LIMIT_BYTES),
        name="oproj_ln",
    )(o2d, h2d, w_o_bf16, g.reshape(1, d), b.reshape(1, d))


def _mlp_ln_kernel(h_ref, wu_ref, wd_ref, g_ref, b_ref, out_ref, *, alpha, d_ff, fc):
    h = h_ref[...]
    hb = h.astype(_BF16)
    acc = alpha * h
    for c in range(d_ff // fc):
        u = jnp.dot(hb, wu_ref[:, c * fc:(c + 1) * fc], preferred_element_type=_F32)
        u = jnp.square(jnp.maximum(u, 0.0)).astype(_BF16)
        acc = acc + jnp.dot(u, wd_ref[c * fc:(c + 1) * fc, :], preferred_element_type=_F32)
    out_ref[...] = _layer_norm(acc, g_ref[...], b_ref[...])


def _mlp_ln(h2d, w_up_bf16, w_down_bf16, g, b, alpha):
    t, d = h2d.shape
    d_ff = w_up_bf16.shape[1]
    tm = min(TOKEN_TILE, t)
    fc = min(FF_CHUNK, d_ff)
    assert t % tm == 0 and d_ff % fc == 0
    return pl.pallas_call(
        functools.partial(_mlp_ln_kernel, alpha=alpha, d_ff=d_ff, fc=fc),
        grid=(t // tm,),
        in_specs=[pl.BlockSpec((tm, d), lambda i: (i, 0)),
                  pl.BlockSpec((d, d_ff), lambda i: (0, 0), pipeline_mode=pl.Buffered(1)),
                  pl.BlockSpec((d_ff, d), lambda i: (0, 0), pipeline_mode=pl.Buffered(1)),
                  pl.BlockSpec((1, d), lambda i: (0, 0)),
                  pl.BlockSpec((1, d), lambda i: (0, 0))],
        out_specs=pl.BlockSpec((tm, d), lambda i: (i, 0)),
        out_shape=jax.ShapeDtypeStruct((t, d), _F32),
        compiler_params=pltpu.CompilerParams(dimension_semantics=("arbitrary",),
                                             vmem_limit_bytes=VMEM_LIMIT_BYTES),
        name="mlp_ln",
    )(h2d, w_up_bf16, w_down_bf16, g.reshape(1, d), b.reshape(1, d))


def _bias_kernel(tab_ref, bias_ref, cfar_ref):
    h = pl.program_id(0)
    w = MOBA_BLOCK
    key = lax.broadcasted_iota(jnp.int32, (w, w), 0)
    qry = lax.broadcasted_iota(jnp.int32, (w, w), 1)
    far = tab_ref[REL_BUCKETS - 1, h] * LOG2E
    for d in range(2):
        dist = qry - key + d * w
        val = jnp.full((w, w), far, _F32)
        for bkt in range(REL_BUCKETS - 2, -1, -1):
            val = jnp.where(dist <= _BUCKET_UB[bkt], tab_ref[bkt, h] * LOG2E, val)
        if d == 0:
            val = jnp.where(dist >= 0, val, _NEG_INF)
        bias_ref[0, d] = val
    cfar_ref[0] = jnp.full((8, w), far, _F32)


def _bias_tiles(rel_bias):
    n_heads = rel_bias.shape[1]
    w = MOBA_BLOCK
    return pl.pallas_call(
        _bias_kernel,
        grid=(n_heads,),
        in_specs=[pl.BlockSpec(memory_space=pltpu.SMEM)],
        out_specs=[pl.BlockSpec((1, 2, w, w), lambda h: (h, 0, 0, 0)),
                   pl.BlockSpec((1, 8, w), lambda h: (h, 0, 0))],
        out_shape=[jax.ShapeDtypeStruct((n_heads, 2, w, w), _F32),
                   jax.ShapeDtypeStruct((n_heads, 8, w), _F32)],
        compiler_params=pltpu.CompilerParams(dimension_semantics=("arbitrary",)),
        name="t5_bias_tiles",
    )(rel_bias.astype(_F32))


def _head_rows(h):
    row = lax.broadcasted_iota(jnp.int32, (LANES, MOBA_BLOCK), 0)
    return (row // HEAD_DIM) == h


def _store_head_queries(q_ref, qT_ref, nb):
    w = MOBA_BLOCK
    for h in range(HEADS_PER_STEP):
        mine = _head_rows(h)
        for i in range(nb):
            q_t = q_ref[:, i * w:(i + 1) * w]
            qT_ref[h, i] = jnp.where(mine, q_t, jnp.zeros_like(q_t))


def _merge_and_store(o_ref, oT_ref, nb):
    w = MOBA_BLOCK
    first = _head_rows(0)
    for i in range(nb):
        both = jnp.where(first, oT_ref[0, i], oT_ref[1, i])
        o_ref[i * w:(i + 1) * w, :] = both.T.astype(_BF16)


def _causal_tiles(nb, newest=None):
    tiles = []
    for i in range(nb):
        keys = list(range(i, -1, -1))
        if newest is not None:
            keys = keys[:newest] if newest > 0 else keys[-newest:]
        tiles += [(h, i, j) for j in keys for h in range(HEADS_PER_STEP)]
    return tiles


def _software_pipeline(n_items, stages, lookahead):
    n_stages = len(stages)
    for step in range(n_items + (n_stages - 1) * lookahead):
        for s, stage in enumerate(stages):
            t = step - s * lookahead
            if 0 <= t < n_items:
                stage(t)


def _moba_kernel(q_ref, k_ref, v_ref, bias_ref, cfar_ref, o_ref,
                 qT_ref, vTh_ref, oT_ref, *, nb, topk):
    w = MOBA_BLOCK
    _store_head_queries(q_ref, qT_ref, nb)

    row = lax.broadcasted_iota(jnp.int32, (LANES, w), 0)
    for h in range(HEADS_PER_STEP):
        ones_row = jnp.where(row == (1 - h) * HEAD_DIM, 1.0, 0.0).astype(_BF16)
        for j in range(nb):
            vTh_ref[h, j] = jnp.where(_head_rows(h), v_ref[:, j * w:(j + 1) * w], ones_row)

    rows = [jnp.mean(k_ref[j * w:(j + 1) * w, :].astype(_F32), axis=0, keepdims=True)
            for j in range(nb)]
    kmean = jnp.concatenate(rows + [jnp.zeros((16 - nb, LANES), _F32)], axis=0)
    p0 = kmean.astype(_BF16)
    r1 = kmean - p0.astype(_F32)
    p1 = r1.astype(_BF16)
    p2 = (r1 - p1.astype(_F32)).astype(_BF16)
    kparts = jnp.concatenate([p0, p1, p2], axis=0)

    jidx = lax.broadcasted_iota(jnp.int32, (8, w), 0)
    addrows = {}
    for h in range(HEADS_PER_STEP):
        cfar = cfar_ref[h][0:1, :]
        for i in range(1, nb):
            g3 = jnp.dot(kparts, qT_ref[h, i], preferred_element_type=_F32)
            gate = g3[0:8] + g3[16:24] + g3[32:40]
            cnt = jnp.zeros((8, w), jnp.int32)
            for jp in range(i):
                rowv = gate[jp:jp + 1, :]
                beats = (rowv > gate) | ((rowv == gate) & (jp < jidx))
                cnt = cnt + jnp.where(beats, 1, 0)
            sel = (cnt < topk) & (jidx < i)
            addrows[h, i] = jnp.where(sel, jnp.where(jidx < i - 1, cfar, 0.0), _NEG_INF)

    tiles = _causal_tiles(nb)
    scores = {}
    state = {}

    def issue_scores(t):
        h, i, j = tiles[t]
        s = jnp.dot(k_ref[j * w:(j + 1) * w, :], qT_ref[h, i], preferred_element_type=_F32)
        if j >= i - 1:
            s = s + bias_ref[h, i - j]
        scores[t] = s

    def softmax_accumulate(t):
        h, i, j = tiles[t]
        s = scores.pop(t)
        mt = jnp.max(s, axis=0, keepdims=True)
        if j == i:
            m = mt
            p = jnp.exp2(s - m)
            acc = jnp.dot(vTh_ref[h, j], p.astype(_BF16), preferred_element_type=_F32)
        else:
            arow = addrows[h, i][j:j + 1, :]
            m_old, acc_old = state[h]
            m = jnp.maximum(m_old, mt + arow)
            a = jnp.exp2(m_old - m)
            p = jnp.exp2(s - (m - arow))
            acc = a * acc_old + jnp.dot(vTh_ref[h, j], p.astype(_BF16),
                                        preferred_element_type=_F32)
        state[h] = (m, acc)
        if j == 0:
            ones_at = (1 - h) * HEAD_DIM
            oT_ref[h, i] = acc * (1.0 / acc[ones_at:ones_at + 1, :])

    _software_pipeline(len(tiles), [issue_scores, softmax_accumulate], MOBA_LOOKAHEAD)
    _merge_and_store(o_ref, oT_ref, nb)


def _attention_call(kernel, q_t, k, v_t, seq, extra_inputs, extra_specs, scratch, name):
    d, t = q_t.shape
    w = MOBA_BLOCK
    assert seq % w == 0 and t % seq == 0 and d % LANES == 0
    nb = seq // w
    assert nb <= 8
    transposed_spec = pl.BlockSpec((LANES, seq), lambda hp, bi: (hp, bi))
    token_major_spec = pl.BlockSpec((seq, LANES), lambda hp, bi: (bi, hp))
    return pl.pallas_call(
        functools.partial(kernel, nb=nb),
        grid=(d // LANES, t // seq),
        in_specs=[transposed_spec, token_major_spec, transposed_spec] + extra_specs,
        out_specs=token_major_spec,
        out_shape=jax.ShapeDtypeStruct((t, d), _BF16),
        scratch_shapes=[pltpu.VMEM((HEADS_PER_STEP, nb, LANES, w), _BF16)]
                       + scratch
                       + [pltpu.VMEM((HEADS_PER_STEP, nb, LANES, w), _F32)],
        compiler_params=pltpu.CompilerParams(
            dimension_semantics=("arbitrary", "arbitrary"),
            vmem_limit_bytes=VMEM_LIMIT_BYTES),
        name=name,
    )(q_t, k, v_t, *extra_inputs)


def _moba_attention(q_t, k, v_t, seq, bias_tiles, cfar):
    w = MOBA_BLOCK
    nb = seq // w
    topk = min(MOBA_TOPK, nb)
    extra_specs = [
        pl.BlockSpec((HEADS_PER_STEP, 2, w, w), lambda hp, bi: (hp, 0, 0, 0)),
        pl.BlockSpec((HEADS_PER_STEP, 8, w), lambda hp, bi: (hp, 0, 0)),
    ]
    return _attention_call(functools.partial(_moba_kernel, topk=topk), q_t, k, v_t, seq,
                           [bias_tiles, cfar], extra_specs,
                           [pltpu.VMEM((HEADS_PER_STEP, nb, LANES, w), _BF16)],
                           "moba_attention")


def _sb_kernel(q_ref, k_ref, v_ref, o_ref, qT_ref, r_ref, oT_ref, *, nb):
    w = MOBA_BLOCK
    _store_head_queries(q_ref, qT_ref, nb)

    key = lax.broadcasted_iota(jnp.int32, (w, w), 0)
    col = lax.broadcasted_iota(jnp.int32, (w, w), 1)
    causal = key < col
    upper = jnp.where(col > key, 1.0, 0.0).astype(_BF16)

    def run_tiles(tiles, resume):
        logits, partial, state, r_final = {}, {}, {}, {}
        first_key = {}
        for h, i, j in tiles:
            first_key.setdefault((h, i), j)
        last_key = {(h, i): j for h, i, j in tiles}

        def issue_logits(t):
            h, i, j = tiles[t]
            logits[t] = jnp.dot(k_ref[j * w:(j + 1) * w, :], qT_ref[h, i],
                                preferred_element_type=_F32)

        def issue_cumsum(t):
            h, i, j = tiles[t]
            z = logits.pop(t)
            neg_abs = pltpu.bitcast(pltpu.bitcast(z, jnp.uint32) | jnp.uint32(0x80000000), _F32)
            sp = jnp.maximum(z, 0.0) + jnp.log(1.0 + jnp.exp2(neg_abs)) * LOG2E
            if j == i:
                sp = jnp.where(causal, sp, 0.0)
            hi = sp.astype(_BF16)
            lo = (sp - hi.astype(_F32)).astype(_BF16)
            after = (jnp.dot(upper, hi, preferred_element_type=_F32)
                     + jnp.dot(upper, lo, preferred_element_type=_F32))
            partial[t] = (z - sp, after, sp[0:1, :])

        def weigh_values(t):
            h, i, j = tiles[t]
            log_sig, after, sp_first = partial.pop(t)
            a = jnp.exp2(log_sig - after)
            if j == i:
                a = jnp.where(causal, a, 0.0)
            pv = jnp.dot(v_ref[:, j * w:(j + 1) * w], a.astype(_BF16),
                         preferred_element_type=_F32)
            tile_total = after[0:1, :] + sp_first
            if j == i:
                acc, r = pv, tile_total
            else:
                if j == first_key[h, i]:
                    acc_old, r_old = oT_ref[h, i], r_ref[h, i]
                else:
                    acc_old, r_old = state[h]
                acc = acc_old + pv * jnp.exp2(-r_old)
                r = r_old + tile_total
            state[h] = (acc, r)
            if j == last_key[h, i]:
                oT_ref[h, i] = acc
                r_final[h, i] = r
                if not resume:
                    r_ref[h, i] = r

        _software_pipeline(len(tiles), [issue_logits, issue_cumsum, weigh_values], SB_LOOKAHEAD)
        return r_final

    r_near = run_tiles(_causal_tiles(nb, newest=SB_NEAR_TILES), resume=False)
    far_tiles = _causal_tiles(nb, newest=-SB_NEAR_TILES)
    if far_tiles:
        r_min = None
        for h, i in sorted({(h, i) for h, i, _ in far_tiles}):
            r_min = r_near[h, i] if r_min is None else jnp.minimum(r_min, r_near[h, i])

        @pl.when(jnp.min(r_min) <= SB_SKIP_LOG2)
        def _():
            run_tiles(far_tiles, resume=True)

    _merge_and_store(o_ref, oT_ref, nb)


def _sb_attention(q_t, k, v_t, seq):
    nb = seq // MOBA_BLOCK
    return _attention_call(_sb_kernel, q_t, k, v_t, seq, [], [],
                           [pltpu.VMEM((HEADS_PER_STEP, nb, 1, MOBA_BLOCK), _F32)],
                           "stick_breaking_attention")


def kernel(x, rel_bias, w_qkv, w_o, ln_mix_g, ln_mix_b, w_up, w_down, ln_ffn_g, ln_ffn_b):
    b, s, d = x.shape
    depth = w_qkv.shape[0]
    alpha = (2.0 * depth) ** 0.25
    assert d % (HEADS_PER_STEP * HEAD_DIM) == 0
    bias_tiles, cfar = _bias_tiles(rel_bias)
    h = x.reshape(b * s, d)
    for i in range(depth):
        q_t, k, v_t = _qkv_proj(h, w_qkv[i])
        if i % 2 == 0:
            o = _moba_attention(q_t, k, v_t, s, bias_tiles, cfar)
        else:
            o = _sb_attention(q_t, k, v_t, s)
        h = _oproj_ln(o, h, w_o[i].astype(_BF16), ln_mix_g[i], ln_mix_b[i], alpha)
        h = _mlp_ln(h, w_up[i].astype(_BF16), w_down[i].astype(_BF16),
                    ln_ffn_g[i], ln_ffn_b[i], alpha)
    return h.reshape(b, s, d)
```

```python
import functools
import math

import numpy as np
import jax
import jax.numpy as jnp
from jax import lax
from jax.experimental import pallas as pl
from jax.experimental.pallas import tpu as pltpu

HEAD_DIM = 64
HEADS_PER_STEP = 2
LANES = 128
MOBA_BLOCK = 256
MOBA_TOPK = 3
REL_BUCKETS = 32
REL_MAX_DIST = 128
LN_EPS = 1e-5
TOKEN_TILE = 512
EPILOGUE_TOKEN_TILE = 1024
EPILOGUE_ROW_SUBTILE = 256
FF_CHUNK = 1024
VMEM_LIMIT_BYTES = 56 * 1024 * 1024
MOBA_LOOKAHEAD = 4
SB_LOOKAHEAD = 2
SB_NEAR_TILES = 2
SB_SKIP_LOG2 = 152.0
LOG2E = math.log2(math.e)

_F32 = jnp.float32
_BF16 = jnp.bfloat16
_NEG_INF = float("-inf")


def _t5_bucket_upper_bounds():
    n = np.arange(0, 4 * REL_MAX_DIST)
    max_exact = REL_BUCKETS // 2
    nf = np.maximum(n, 1).astype(np.float64)
    large = max_exact + (np.log(nf / max_exact) / math.log(REL_MAX_DIST / max_exact)
                         * (REL_BUCKETS - max_exact)).astype(np.int64)
    large = np.minimum(large, REL_BUCKETS - 1)
    bucket = np.where(n < max_exact, n, large)
    assert bucket[-1] == REL_BUCKETS - 1 and np.all(np.diff(bucket) >= 0)
    return tuple(int(n[bucket == b].max()) for b in range(REL_BUCKETS - 1))


_BUCKET_UB = _t5_bucket_upper_bounds()
assert _BUCKET_UB[-1] < MOBA_BLOCK


def _layer_norm(y, g, b):
    mu = jnp.mean(y, axis=-1, keepdims=True)
    yc = y - mu
    var = jnp.mean(yc * yc, axis=-1, keepdims=True)
    return yc * lax.rsqrt(var + LN_EPS) * g + b


_CONTRACT_LAST = (((1,), (1,)), ((), ()))


def _qkv_kernel(x_ref, wqT_ref, wk_ref, wvT_ref, qT_ref, k_ref, vT_ref, *, q_scale):
    xb = x_ref[...].astype(_BF16)
    q_t = lax.dot_general(wqT_ref[...], xb, _CONTRACT_LAST, preferred_element_type=_F32)
    qT_ref[...] = (q_t * q_scale).astype(_BF16)
    k_ref[...] = jnp.dot(xb, wk_ref[...], preferred_element_type=_F32).astype(_BF16)
    v_t = lax.dot_general(wvT_ref[...], xb, _CONTRACT_LAST, preferred_element_type=_F32)
    vT_ref[...] = v_t.astype(_BF16)


def _qkv_proj(h2d, w_qkv):
    t, d = h2d.shape
    tm = min(TOKEN_TILE, t)
    assert t % tm == 0 and w_qkv.shape == (d, 3 * d)
    wq_t = w_qkv[:, :d].T.astype(_BF16)
    wk = w_qkv[:, d:2 * d].astype(_BF16)
    wv_t = w_qkv[:, 2 * d:].T.astype(_BF16)
    weight_spec = pl.BlockSpec((d, d), lambda i: (0, 0), pipeline_mode=pl.Buffered(1))
    transposed = jax.ShapeDtypeStruct((d, t), _BF16)
    return pl.pallas_call(
        functools.partial(_qkv_kernel, q_scale=LOG2E * HEAD_DIM ** -0.5),
        grid=(t // tm,),
        in_specs=[pl.BlockSpec((tm, d), lambda i: (i, 0)), weight_spec, weight_spec, weight_spec],
        out_specs=[pl.BlockSpec((d, tm), lambda i: (0, i)),
                   pl.BlockSpec((tm, d), lambda i: (i, 0)),
                   pl.BlockSpec((d, tm), lambda i: (0, i))],
        out_shape=[transposed, jax.ShapeDtypeStruct((t, d), _BF16), transposed],
        compiler_params=pltpu.CompilerParams(dimension_semantics=("arbitrary",),
                                             vmem_limit_bytes=VMEM_LIMIT_BYTES),
        name="qkv_proj",
    )(h2d, wq_t, wk, wv_t)


def _oproj_ln_kernel(o_ref, h_ref, w_ref, g_ref, b_ref, out_ref, *, alpha, sub):
    accs = {}

    def project(s):
        accs[s] = jnp.dot(o_ref[s * sub:(s + 1) * sub, :], w_ref[...],
                          preferred_element_type=_F32)

    def normalize(s):
        rows = slice(s * sub, (s + 1) * sub)
        out_ref[rows, :] = _layer_norm(alpha * h_ref[rows, :] + accs.pop(s),
                                       g_ref[...], b_ref[...])

    _software_pipeline(o_ref.shape[0] // sub, [project, normalize], 1)


def _oproj_ln(o2d, h2d, w_o_bf16, g, b, alpha):
    t, d = h2d.shape
    tm = min(EPILOGUE_TOKEN_TILE, t)
    sub = min(EPILOGUE_ROW_SUBTILE, tm)
    assert t % tm == 0 and tm % sub == 0
    return pl.pallas_call(
        functools.partial(_oproj_ln_kernel, alpha=alpha, sub=sub),
        grid=(t // tm,),
        in_specs=[pl.BlockSpec((tm, d), lambda i: (i, 0)),
                  pl.BlockSpec((tm, d), lambda i: (i, 0)),
                  pl.BlockSpec((d, d), lambda i: (0, 0), pipeline_mode=pl.Buffered(1)),
                  pl.BlockSpec((1, d), lambda i: (0, 0)),
                  pl.BlockSpec((1, d), lambda i: (0, 0))],
        out_specs=pl.BlockSpec((tm, d), lambda i: (i, 0)),
        out_shape=jax.ShapeDtypeStruct((t, d), _F32),
        compiler_params=pltpu.CompilerParams(dimension_semantics=("arbitrary",),
                                             vmem_limit_bytes=VMEM_LIMIT_BYTES),
        name="oproj_ln",
    )(o2d, h2d, w_o_bf16, g.reshape(1, d), b.reshape(1, d))


def _mlp_ln_kernel(h_ref, wu_ref, wd_ref, g_ref, b_ref, out_ref, *, alpha, d_ff, fc, sub):
    accs = {}

    def mix_channels(s):
        h = h_ref[s * sub:(s + 1) * sub, :]
        hb = h.astype(_BF16)
        acc = alpha * h
        for c in range(d_ff // fc):
            u = jnp.dot(hb, wu_ref[:, c * fc:(c + 1) * fc], preferred_element_type=_F32)
            u = jnp.square(jnp.maximum(u, 0.0)).astype(_BF16)
            acc = acc + jnp.dot(u, wd_ref[c * fc:(c + 1) * fc, :], preferred_element_type=_F32)
        accs[s] = acc

    def normalize(s):
        out_ref[s * sub:(s + 1) * sub, :] = _layer_norm(accs.pop(s), g_ref[...], b_ref[...])

    _software_pipeline(h_ref.shape[0] // sub, [mix_channels, normalize], 1)


def _mlp_ln(h2d, w_up_bf16, w_down_bf16, g, b, alpha):
    t, d = h2d.shape
    d_ff = w_up_bf16.shape[1]
    tm = min(EPILOGUE_TOKEN_TILE, t)
    sub = min(EPILOGUE_ROW_SUBTILE, tm)
    fc = min(FF_CHUNK, d_ff)
    assert t % tm == 0 and tm % sub == 0 and d_ff % fc == 0
    return pl.pallas_call(
        functools.partial(_mlp_ln_kernel, alpha=alpha, d_ff=d_ff, fc=fc, sub=sub),
        grid=(t // tm,),
        in_specs=[pl.BlockSpec((tm, d), lambda i: (i, 0)),
                  pl.BlockSpec((d, d_ff), lambda i: (0, 0), pipeline_mode=pl.Buffered(1)),
                  pl.BlockSpec((d_ff, d), lambda i: (0, 0), pipeline_mode=pl.Buffered(1)),
                  pl.BlockSpec((1, d), lambda i: (0, 0)),
                  pl.BlockSpec((1, d), lambda i: (0, 0))],
        out_specs=pl.BlockSpec((tm, d), lambda i: (i, 0)),
        out_shape=jax.ShapeDtypeStruct((t, d), _F32),
        compiler_params=pltpu.CompilerParams(dimension_semantics=("arbitrary",),
                                             vmem_limit_bytes=VMEM_LIMIT_BYTES),
        name="mlp_ln",
    )(h2d, w_up_bf16, w_down_bf16, g.reshape(1, d), b.reshape(1, d))


def _bias_kernel(tab_ref, bias_ref, cfar_ref):
    h = pl.program_id(0)
    w = MOBA_BLOCK
    key = lax.broadcasted_iota(jnp.int32, (w, w), 0)
    qry = lax.broadcasted_iota(jnp.int32, (w, w), 1)
    far = tab_ref[REL_BUCKETS - 1, h] * LOG2E
    for d in range(2):
        dist = qry - key + d * w
        val = jnp.full((w, w), far, _F32)
        for bkt in range(REL_BUCKETS - 2, -1, -1):
            val = jnp.where(dist <= _BUCKET_UB[bkt], tab_ref[bkt, h] * LOG2E, val)
        if d == 0:
            val = jnp.where(dist >= 0, val, _NEG_INF)
        bias_ref[0, d] = val
    cfar_ref[0] = jnp.full((8, w), far, _F32)


def _bias_tiles(rel_bias):
    n_heads = rel_bias.shape[1]
    w = MOBA_BLOCK
    return pl.pallas_call(
        _bias_kernel,
        grid=(n_heads,),
        in_specs=[pl.BlockSpec(memory_space=pltpu.SMEM)],
        out_specs=[pl.BlockSpec((1, 2, w, w), lambda h: (h, 0, 0, 0)),
                   pl.BlockSpec((1, 8, w), lambda h: (h, 0, 0))],
        out_shape=[jax.ShapeDtypeStruct((n_heads, 2, w, w), _F32),
                   jax.ShapeDtypeStruct((n_heads, 8, w), _F32)],
        compiler_params=pltpu.CompilerParams(dimension_semantics=("arbitrary",)),
        name="t5_bias_tiles",
    )(rel_bias.astype(_F32))


def _head_rows(h):
    row = lax.broadcasted_iota(jnp.int32, (LANES, MOBA_BLOCK), 0)
    return (row // HEAD_DIM) == h


def _store_head_queries(q_ref, qT_ref, nb):
    w = MOBA_BLOCK
    for h in range(HEADS_PER_STEP):
        mine = _head_rows(h)
        for i in range(nb):
            q_t = q_ref[:, i * w:(i + 1) * w]
            qT_ref[h, i] = jnp.where(mine, q_t, jnp.zeros_like(q_t))


def _merge_and_store(o_ref, oT_ref, nb):
    w = MOBA_BLOCK
    first = _head_rows(0)
    for i in range(nb):
        both = jnp.where(first, oT_ref[0, i], oT_ref[1, i])
        o_ref[i * w:(i + 1) * w, :] = both.T.astype(_BF16)


def _causal_tiles(nb, newest=None):
    tiles = []
    for i in range(nb):
        keys = list(range(i, -1, -1))
        if newest is not None:
            keys = keys[:newest] if newest > 0 else keys[-newest:]
        tiles += [(h, i, j) for j in keys for h in range(HEADS_PER_STEP)]
    return tiles


def _software_pipeline(n_items, stages, lookahead):
    n_stages = len(stages)
    for step in range(n_items + (n_stages - 1) * lookahead):
        for s, stage in enumerate(stages):
            t = step - s * lookahead
            if 0 <= t < n_items:
                stage(t)


def _moba_kernel(q_ref, k_ref, v_ref, bias_ref, cfar_ref, o_ref,
                 qT_ref, vTh_ref, oT_ref, *, nb, topk):
    w = MOBA_BLOCK
    _store_head_queries(q_ref, qT_ref, nb)

    row = lax.broadcasted_iota(jnp.int32, (LANES, w), 0)
    for h in range(HEADS_PER_STEP):
        ones_row = jnp.where(row == (1 - h) * HEAD_DIM, 1.0, 0.0).astype(_BF16)
        for j in range(nb):
            vTh_ref[h, j] = jnp.where(_head_rows(h), v_ref[:, j * w:(j + 1) * w], ones_row)

    rows = [jnp.mean(k_ref[j * w:(j + 1) * w, :].astype(_F32), axis=0, keepdims=True)
            for j in range(nb)]
    kmean = jnp.concatenate(rows + [jnp.zeros((16 - nb, LANES), _F32)], axis=0)
    p0 = kmean.astype(_BF16)
    r1 = kmean - p0.astype(_F32)
    p1 = r1.astype(_BF16)
    p2 = (r1 - p1.astype(_F32)).astype(_BF16)
    kparts = jnp.concatenate([p0, p1, p2], axis=0)

    jidx = lax.broadcasted_iota(jnp.int32, (8, w), 0)
    addrows = {}
    for h in range(HEADS_PER_STEP):
        cfar = cfar_ref[h][0:1, :]
        for i in range(1, nb):
            g3 = jnp.dot(kparts, qT_ref[h, i], preferred_element_type=_F32)
            gate = g3[0:8] + g3[16:24] + g3[32:40]
            cnt = jnp.zeros((8, w), jnp.int32)
            for jp in range(i):
                rowv = gate[jp:jp + 1, :]
                beats = (rowv > gate) | ((rowv == gate) & (jp < jidx))
                cnt = cnt + jnp.where(beats, 1, 0)
            sel = (cnt < topk) & (jidx < i)
            addrows[h, i] = jnp.where(sel, jnp.where(jidx < i - 1, cfar, 0.0), _NEG_INF)

    tiles = _causal_tiles(nb)
    scores = {}
    state = {}

    def issue_scores(t):
        h, i, j = tiles[t]
        s = jnp.dot(k_ref[j * w:(j + 1) * w, :], qT_ref[h, i], preferred_element_type=_F32)
        if j >= i - 1:
            s = s + bias_ref[h, i - j]
        scores[t] = s

    def softmax_accumulate(t):
        h, i, j = tiles[t]
        s = scores.pop(t)
        mt = jnp.max(s, axis=0, keepdims=True)
        if j == i:
            m = mt
            p = jnp.exp2(s - m)
            acc = jnp.dot(vTh_ref[h, j], p.astype(_BF16), preferred_element_type=_F32)
        else:
            arow = addrows[h, i][j:j + 1, :]
            m_old, acc_old = state[h]
            m = jnp.maximum(m_old, mt + arow)
            a = jnp.exp2(m_old - m)
            p = jnp.exp2(s - (m - arow))
            acc = a * acc_old + jnp.dot(vTh_ref[h, j], p.astype(_BF16),
                                        preferred_element_type=_F32)
        state[h] = (m, acc)
        if j == 0:
            ones_at = (1 - h) * HEAD_DIM
            oT_ref[h, i] = acc * (1.0 / acc[ones_at:ones_at + 1, :])

    _software_pipeline(len(tiles), [issue_scores, softmax_accumulate], MOBA_LOOKAHEAD)
    _merge_and_store(o_ref, oT_ref, nb)


def _attention_call(kernel, q_t, k, v_t, seq, extra_inputs, extra_specs, scratch, name):
    d, t = q_t.shape
    w = MOBA_BLOCK
    assert seq % w == 0 and t % seq == 0 and d % LANES == 0
    nb = seq // w
    assert nb <= 8
    transposed_spec = pl.BlockSpec((LANES, seq), lambda hp, bi: (hp, bi))
    token_major_spec = pl.BlockSpec((seq, LANES), lambda hp, bi: (bi, hp))
    return pl.pallas_call(
        functools.partial(kernel, nb=nb),
        grid=(d // LANES, t // seq),
        in_specs=[transposed_spec, token_major_spec, transposed_spec] + extra_specs,
        out_specs=token_major_spec,
        out_shape=jax.ShapeDtypeStruct((t, d), _BF16),
        scratch_shapes=[pltpu.VMEM((HEADS_PER_STEP, nb, LANES, w), _BF16)]
                       + scratch
                       + [pltpu.VMEM((HEADS_PER_STEP, nb, LANES, w), _F32)],
        compiler_params=pltpu.CompilerParams(
            dimension_semantics=("arbitrary", "arbitrary"),
            vmem_limit_bytes=VMEM_LIMIT_BYTES),
        name=name,
    )(q_t, k, v_t, *extra_inputs)


def _moba_attention(q_t, k, v_t, seq, bias_tiles, cfar):
    w = MOBA_BLOCK
    nb = seq // w
    topk = min(MOBA_TOPK, nb)
    extra_specs = [
        pl.BlockSpec((HEADS_PER_STEP, 2, w, w), lambda hp, bi: (hp, 0, 0, 0)),
        pl.BlockSpec((HEADS_PER_STEP, 8, w), lambda hp, bi: (hp, 0, 0)),
    ]
    return _attention_call(functools.partial(_moba_kernel, topk=topk), q_t, k, v_t, seq,
                           [bias_tiles, cfar], extra_specs,
                           [pltpu.VMEM((HEADS_PER_STEP, nb, LANES, w), _BF16)],
                           "moba_attention")


def _sb_kernel(q_ref, k_ref, v_ref, o_ref, qT_ref, r_ref, oT_ref, *, nb):
    w = MOBA_BLOCK
    _store_head_queries(q_ref, qT_ref, nb)

    key = lax.broadcasted_iota(jnp.int32, (w, w), 0)
    col = lax.broadcasted_iota(jnp.int32, (w, w), 1)
    causal = key < col
    upper = jnp.where(col > key, 1.0, 0.0).astype(_BF16)

    def run_tiles(tiles, resume):
        logits, partial, state, r_final = {}, {}, {}, {}
        first_key = {}
        for h, i, j in tiles:
            first_key.setdefault((h, i), j)
        last_key = {(h, i): j for h, i, j in tiles}

        def issue_logits(t):
            h, i, j = tiles[t]
            logits[t] = jnp.dot(k_ref[j * w:(j + 1) * w, :], qT_ref[h, i],
                                preferred_element_type=_F32)

        def issue_cumsum(t):
            h, i, j = tiles[t]
            z = logits.pop(t)
            neg_abs = pltpu.bitcast(pltpu.bitcast(z, jnp.uint32) | jnp.uint32(0x80000000), _F32)
            sp = jnp.maximum(z, 0.0) + jnp.log(1.0 + jnp.exp2(neg_abs)) * LOG2E
            if j == i:
                sp = jnp.where(causal, sp, 0.0)
            hi = sp.astype(_BF16)
            lo = (sp - hi.astype(_F32)).astype(_BF16)
            after = (jnp.dot(upper, hi, preferred_element_type=_F32)
                     + jnp.dot(upper, lo, preferred_element_type=_F32))
            partial[t] = (z - sp, after, sp[0:1, :])

        def weigh_values(t):
            h, i, j = tiles[t]
            log_sig, after, sp_first = partial.pop(t)
            a = jnp.exp2(log_sig - after)
            if j == i:
                a = jnp.where(causal, a, 0.0)
            pv = jnp.dot(v_ref[:, j * w:(j + 1) * w], a.astype(_BF16),
                         preferred_element_type=_F32)
            tile_total = after[0:1, :] + sp_first
            if j == i:
                acc, r = pv, tile_total
            else:
                if j == first_key[h, i]:
                    acc_old, r_old = oT_ref[h, i], r_ref[h, i]
                else:
                    acc_old, r_old = state[h]
                acc = acc_old + pv * jnp.exp2(-r_old)
                r = r_old + tile_total
            state[h] = (acc, r)
            if j == last_key[h, i]:
                oT_ref[h, i] = acc
                r_final[h, i] = r
                if not resume:
                    r_ref[h, i] = r

        _software_pipeline(len(tiles), [issue_logits, issue_cumsum, weigh_values], SB_LOOKAHEAD)
        return r_final

    r_near = run_tiles(_causal_tiles(nb, newest=SB_NEAR_TILES), resume=False)
    far_tiles = _causal_tiles(nb, newest=-SB_NEAR_TILES)
    if far_tiles:
        r_min = None
        for h, i in sorted({(h, i) for h, i, _ in far_tiles}):
            r_min = r_near[h, i] if r_min is None else jnp.minimum(r_min, r_near[h, i])

        @pl.when(jnp.min(r_min) <= SB_SKIP_LOG2)
        def _():
            run_tiles(far_tiles, resume=True)

    _merge_and_store(o_ref, oT_ref, nb)


def _sb_attention(q_t, k, v_t, seq):
    nb = seq // MOBA_BLOCK
    return _attention_call(_sb_kernel, q_t, k, v_t, seq, [], [],
                           [pltpu.VMEM((HEADS_PER_STEP, nb, 1, MOBA_BLOCK), _F32)],
                           "stick_breaking_attention")


def kernel(x, rel_bias, w_qkv, w_o, ln_mix_g, ln_mix_b, w_up, w_down, ln_ffn_g, ln_ffn_b):
    b, s, d = x.shape
    depth = w_qkv.shape[0]
    alpha = (2.0 * depth) ** 0.25
    assert d % (HEADS_PER_STEP * HEAD_DIM) == 0
    bias_tiles, cfar = _bias_tiles(rel_bias)
    h = x.reshape(b * s, d)
    for i in range(depth):
        q_t, k, v_t = _qkv_proj(h, w_qkv[i])
        if i % 2 == 0:
            o = _moba_attention(q_t, k, v_t, s, bias_tiles, cfar)
        else:
            o = _sb_attention(q_t, k, v_t, s)
        h = _oproj_ln(o, h, w_o[i].astype(_BF16), ln_mix_g[i], ln_mix_b[i], alpha)
        h = _mlp_ln(h, w_up[i].astype(_BF16), w_down[i].astype(_BF16),
                    ln_ffn_g[i], ln_ffn_b[i], alpha)
    return h.reshape(b, s, d)
```

```python
import functools
import math

import numpy as np
import jax
import jax.numpy as jnp
from jax import lax
from jax.experimental import pallas as pl
from jax.experimental.pallas import tpu as pltpu

HEAD_DIM = 64
HEADS_PER_STEP = 2
LANES = 128
MOBA_BLOCK = 256
MOBA_TOPK = 3
REL_BUCKETS = 32
REL_MAX_DIST = 128
LN_EPS = 1e-5
TOKEN_TILE = 512
EPILOGUE_TOKEN_TILE = 1024
EPILOGUE_ROW_SUBTILE = 256
FF_CHUNK = 1024
VMEM_LIMIT_BYTES = 56 * 1024 * 1024
MOBA_LOOKAHEAD = 6
SB_LOOKAHEAD = 2
SB_NEAR_TILES = 2
SB_SKIP_LOG2 = 152.0
LOG2E = math.log2(math.e)

_F32 = jnp.float32
_BF16 = jnp.bfloat16
_NEG_INF = float("-inf")


def _t5_bucket_upper_bounds():
    n = np.arange(0, 4 * REL_MAX_DIST)
    max_exact = REL_BUCKETS // 2
    nf = np.maximum(n, 1).astype(np.float64)
    large = max_exact + (np.log(nf / max_exact) / math.log(REL_MAX_DIST / max_exact)
                         * (REL_BUCKETS - max_exact)).astype(np.int64)
    large = np.minimum(large, REL_BUCKETS - 1)
    bucket = np.where(n < max_exact, n, large)
    assert bucket[-1] == REL_BUCKETS - 1 and np.all(np.diff(bucket) >= 0)
    return tuple(int(n[bucket == b].max()) for b in range(REL_BUCKETS - 1))


_BUCKET_UB = _t5_bucket_upper_bounds()
assert _BUCKET_UB[-1] < MOBA_BLOCK


def _layer_norm(y, g, b):
    mu = jnp.mean(y, axis=-1, keepdims=True)
    yc = y - mu
    var = jnp.mean(yc * yc, axis=-1, keepdims=True)
    return yc * lax.rsqrt(var + LN_EPS) * g + b


_CONTRACT_LAST = (((1,), (1,)), ((), ()))


def _qkv_kernel(x_ref, wqT_ref, wk_ref, wvT_ref, qT_ref, k_ref, vT_ref, *, q_scale):
    xb = x_ref[...].astype(_BF16)
    q_t = lax.dot_general(wqT_ref[...], xb, _CONTRACT_LAST, preferred_element_type=_F32)
    qT_ref[...] = (q_t * q_scale).astype(_BF16)
    k_ref[...] = jnp.dot(xb, wk_ref[...], preferred_element_type=_F32).astype(_BF16)
    v_t = lax.dot_general(wvT_ref[...], xb, _CONTRACT_LAST, preferred_element_type=_F32)
    vT_ref[...] = v_t.astype(_BF16)


def _qkv_proj(h2d, w_qkv):
    t, d = h2d.shape
    tm = min(TOKEN_TILE, t)
    assert t % tm == 0 and w_qkv.shape == (d, 3 * d)
    wq_t = w_qkv[:, :d].T.astype(_BF16)
    wk = w_qkv[:, d:2 * d].astype(_BF16)
    wv_t = w_qkv[:, 2 * d:].T.astype(_BF16)
    weight_spec = pl.BlockSpec((d, d), lambda i: (0, 0), pipeline_mode=pl.Buffered(1))
    transposed = jax.ShapeDtypeStruct((d, t), _BF16)
    return pl.pallas_call(
        functools.partial(_qkv_kernel, q_scale=LOG2E * HEAD_DIM ** -0.5),
        grid=(t // tm,),
        in_specs=[pl.BlockSpec((tm, d), lambda i: (i, 0)), weight_spec, weight_spec, weight_spec],
        out_specs=[pl.BlockSpec((d, tm), lambda i: (0, i)),
                   pl.BlockSpec((tm, d), lambda i: (i, 0)),
                   pl.BlockSpec((d, tm), lambda i: (0, i))],
        out_shape=[transposed, jax.ShapeDtypeStruct((t, d), _BF16), transposed],
        compiler_params=pltpu.CompilerParams(dimension_semantics=("arbitrary",),
                                             vmem_limit_bytes=VMEM_LIMIT_BYTES),
        name="qkv_proj",
    )(h2d, wq_t, wk, wv_t)


def _oproj_ln_kernel(o_ref, h_ref, w_ref, g_ref, b_ref, out_ref, *, alpha, sub):
    accs = {}

    def project(s):
        accs[s] = jnp.dot(o_ref[s * sub:(s + 1) * sub, :], w_ref[...],
                          preferred_element_type=_F32)

    def normalize(s):
        rows = slice(s * sub, (s + 1) * sub)
        out_ref[rows, :] = _layer_norm(alpha * h_ref[rows, :] + accs.pop(s),
                                       g_ref[...], b_ref[...])

    _software_pipeline(o_ref.shape[0] // sub, [project, normalize], 1)


def _oproj_ln(o2d, h2d, w_o_bf16, g, b, alpha):
    t, d = h2d.shape
    tm = min(EPILOGUE_TOKEN_TILE, t)
    sub = min(EPILOGUE_ROW_SUBTILE, tm)
    assert t % tm == 0 and tm % sub == 0
    return pl.pallas_call(
        functools.partial(_oproj_ln_kernel, alpha=alpha, sub=sub),
        grid=(t // tm,),
        in_specs=[pl.BlockSpec((tm, d), lambda i: (i, 0)),
                  pl.BlockSpec((tm, d), lambda i: (i, 0)),
                  pl.BlockSpec((d, d), lambda i: (0, 0), pipeline_mode=pl.Buffered(1)),
                  pl.BlockSpec((1, d), lambda i: (0, 0)),
                  pl.BlockSpec((1, d), lambda i: (0, 0))],
        out_specs=pl.BlockSpec((tm, d), lambda i: (i, 0)),
        out_shape=jax.ShapeDtypeStruct((t, d), _F32),
        compiler_params=pltpu.CompilerParams(dimension_semantics=("arbitrary",),
                                             vmem_limit_bytes=VMEM_LIMIT_BYTES),
        name="oproj_ln",
    )(o2d, h2d, w_o_bf16, g.reshape(1, d), b.reshape(1, d))


def _mlp_ln_kernel(h_ref, wu_ref, wd_ref, g_ref, b_ref, out_ref, *, alpha, d_ff, fc, sub):
    accs = {}

    def mix_channels(s):
        h = h_ref[s * sub:(s + 1) * sub, :]
        hb = h.astype(_BF16)
        acc = alpha * h
        for c in range(d_ff // fc):
            u = jnp.dot(hb, wu_ref[:, c * fc:(c + 1) * fc], preferred_element_type=_F32)
            u = jnp.square(jnp.maximum(u, 0.0)).astype(_BF16)
            acc = acc + jnp.dot(u, wd_ref[c * fc:(c + 1) * fc, :], preferred_element_type=_F32)
        accs[s] = acc

    def normalize(s):
        out_ref[s * sub:(s + 1) * sub, :] = _layer_norm(accs.pop(s), g_ref[...], b_ref[...])

    _software_pipeline(h_ref.shape[0] // sub, [mix_channels, normalize], 1)


def _mlp_ln(h2d, w_up_bf16, w_down_bf16, g, b, alpha):
    t, d = h2d.shape
    d_ff = w_up_bf16.shape[1]
    tm = min(EPILOGUE_TOKEN_TILE, t)
    sub = min(EPILOGUE_ROW_SUBTILE, tm)
    fc = min(FF_CHUNK, d_ff)
    assert t % tm == 0 and tm % sub == 0 and d_ff % fc == 0
    return pl.pallas_call(
        functools.partial(_mlp_ln_kernel, alpha=alpha, d_ff=d_ff, fc=fc, sub=sub),
        grid=(t // tm,),
        in_specs=[pl.BlockSpec((tm, d), lambda i: (i, 0)),
                  pl.BlockSpec((d, d_ff), lambda i: (0, 0), pipeline_mode=pl.Buffered(1)),
                  pl.BlockSpec((d_ff, d), lambda i: (0, 0), pipeline_mode=pl.Buffered(1)),
                  pl.BlockSpec((1, d), lambda i: (0, 0)),
                  pl.BlockSpec((1, d), lambda i: (0, 0))],
        out_specs=pl.BlockSpec((tm, d), lambda i: (i, 0)),
        out_shape=jax.ShapeDtypeStruct((t, d), _F32),
        compiler_params=pltpu.CompilerParams(dimension_semantics=("arbitrary",),
                                             vmem_limit_bytes=VMEM_LIMIT_BYTES),
        name="mlp_ln",
    )(h2d, w_up_bf16, w_down_bf16, g.reshape(1, d), b.reshape(1, d))


def _bias_kernel(tab_ref, bias_ref, cfar_ref):
    h = pl.program_id(0)
    w = MOBA_BLOCK
    key = lax.broadcasted_iota(jnp.int32, (w, w), 0)
    qry = lax.broadcasted_iota(jnp.int32, (w, w), 1)
    far = tab_ref[REL_BUCKETS - 1, h] * LOG2E
    for d in range(2):
        dist = qry - key + d * w
        val = jnp.full((w, w), far, _F32)
        for bkt in range(REL_BUCKETS - 2, -1, -1):
            val = jnp.where(dist <= _BUCKET_UB[bkt], tab_ref[bkt, h] * LOG2E, val)
        if d == 0:
            val = jnp.where(dist >= 0, val, _NEG_INF)
        bias_ref[0, d] = val
    cfar_ref[0] = jnp.full((8, w), far, _F32)


def _bias_tiles(rel_bias):
    n_heads = rel_bias.shape[1]
    w = MOBA_BLOCK
    return pl.pallas_call(
        _bias_kernel,
        grid=(n_heads,),
        in_specs=[pl.BlockSpec(memory_space=pltpu.SMEM)],
        out_specs=[pl.BlockSpec((1, 2, w, w), lambda h: (h, 0, 0, 0)),
                   pl.BlockSpec((1, 8, w), lambda h: (h, 0, 0))],
        out_shape=[jax.ShapeDtypeStruct((n_heads, 2, w, w), _F32),
                   jax.ShapeDtypeStruct((n_heads, 8, w), _F32)],
        compiler_params=pltpu.CompilerParams(dimension_semantics=("arbitrary",)),
        name="t5_bias_tiles",
    )(rel_bias.astype(_F32))


def _head_rows(h):
    row = lax.broadcasted_iota(jnp.int32, (LANES, MOBA_BLOCK), 0)
    return (row // HEAD_DIM) == h


def _store_head_queries(q_ref, qT_ref, nb):
    w = MOBA_BLOCK
    for h in range(HEADS_PER_STEP):
        mine = _head_rows(h)
        for i in range(nb):
            q_t = q_ref[:, i * w:(i + 1) * w]
            qT_ref[h, i] = jnp.where(mine, q_t, jnp.zeros_like(q_t))


def _merge_and_store(o_ref, oT_ref, nb):
    w = MOBA_BLOCK
    first = _head_rows(0)
    for i in range(nb):
        both = jnp.where(first, oT_ref[0, i], oT_ref[1, i])
        o_ref[i * w:(i + 1) * w, :] = both.T.astype(_BF16)


def _causal_tiles(nb, newest=None):
    tiles = []
    for i in range(nb):
        keys = list(range(i, -1, -1))
        if newest is not None:
            keys = keys[:newest] if newest > 0 else keys[-newest:]
        tiles += [(h, i, j) for j in keys for h in range(HEADS_PER_STEP)]
    return tiles


def _software_pipeline(n_items, stages, lookahead):
    n_stages = len(stages)
    for step in range(n_items + (n_stages - 1) * lookahead):
        for s, stage in enumerate(stages):
            t = step - s * lookahead
            if 0 <= t < n_items:
                stage(t)


def _moba_kernel(q_ref, k_ref, v_ref, bias_ref, cfar_ref, o_ref,
                 qT_ref, vTh_ref, oT_ref, *, nb, topk):
    w = MOBA_BLOCK
    _store_head_queries(q_ref, qT_ref, nb)

    row = lax.broadcasted_iota(jnp.int32, (LANES, w), 0)
    for h in range(HEADS_PER_STEP):
        ones_row = jnp.where(row == (1 - h) * HEAD_DIM, 1.0, 0.0).astype(_BF16)
        for j in range(nb):
            vTh_ref[h, j] = jnp.where(_head_rows(h), v_ref[:, j * w:(j + 1) * w], ones_row)

    rows = [jnp.mean(k_ref[j * w:(j + 1) * w, :].astype(_F32), axis=0, keepdims=True)
            for j in range(nb)]
    kmean = jnp.concatenate(rows + [jnp.zeros((16 - nb, LANES), _F32)], axis=0)
    p0 = kmean.astype(_BF16)
    r1 = kmean - p0.astype(_F32)
    p1 = r1.astype(_BF16)
    p2 = (r1 - p1.astype(_F32)).astype(_BF16)
    kparts = jnp.concatenate([p0, p1, p2], axis=0)

    jidx = lax.broadcasted_iota(jnp.int32, (8, w), 0)
    addrows = {}
    for h in range(HEADS_PER_STEP):
        cfar = cfar_ref[h][0:1, :]
        for i in range(1, nb):
            g3 = jnp.dot(kparts, qT_ref[h, i], preferred_element_type=_F32)
            gate = g3[0:8] + g3[16:24] + g3[32:40]
            cnt = jnp.zeros((8, w), jnp.int32)
            for jp in range(i):
                rowv = gate[jp:jp + 1, :]
                beats = (rowv > gate) | ((rowv == gate) & (jp < jidx))
                cnt = cnt + jnp.where(beats, 1, 0)
            sel = (cnt < topk) & (jidx < i)
            addrows[h, i] = jnp.where(sel, jnp.where(jidx < i - 1, cfar, 0.0), _NEG_INF)

    tiles = _causal_tiles(nb)
    scores = {}
    state = {}

    def issue_scores(t):
        h, i, j = tiles[t]
        s = jnp.dot(k_ref[j * w:(j + 1) * w, :], qT_ref[h, i], preferred_element_type=_F32)
        if j >= i - 1:
            s = s + bias_ref[h, i - j]
        scores[t] = s

    def softmax_accumulate(t):
        h, i, j = tiles[t]
        s = scores.pop(t)
        mt = jnp.max(s, axis=0, keepdims=True)
        if j == i:
            m = mt
            p = jnp.exp2(s - m)
            acc = jnp.dot(vTh_ref[h, j], p.astype(_BF16), preferred_element_type=_F32)
        else:
            arow = addrows[h, i][j:j + 1, :]
            m_old, acc_old = state[h]
            m = jnp.maximum(m_old, mt + arow)
            a = jnp.exp2(m_old - m)
            p = jnp.exp2(s - (m - arow))
            acc = a * acc_old + jnp.dot(vTh_ref[h, j], p.astype(_BF16),
                                        preferred_element_type=_F32)
        state[h] = (m, acc)
        if j == 0:
            ones_at = (1 - h) * HEAD_DIM
            oT_ref[h, i] = acc * (1.0 / acc[ones_at:ones_at + 1, :])

    _software_pipeline(len(tiles), [issue_scores, softmax_accumulate], MOBA_LOOKAHEAD)
    _merge_and_store(o_ref, oT_ref, nb)


def _attention_call(kernel, q_t, k, v_t, seq, extra_inputs, extra_specs, scratch, name):
    d, t = q_t.shape
    w = MOBA_BLOCK
    assert seq % w == 0 and t % seq == 0 and d % LANES == 0
    nb = seq // w
    assert nb <= 8
    transposed_spec = pl.BlockSpec((LANES, seq), lambda hp, bi: (hp, bi))
    token_major_spec = pl.BlockSpec((seq, LANES), lambda hp, bi: (bi, hp))
    return pl.pallas_call(
        functools.partial(kernel, nb=nb),
        grid=(d // LANES, t // seq),
        in_specs=[transposed_spec, token_major_spec, transposed_spec] + extra_specs,
        out_specs=token_major_spec,
        out_shape=jax.ShapeDtypeStruct((t, d), _BF16),
        scratch_shapes=[pltpu.VMEM((HEADS_PER_STEP, nb, LANES, w), _BF16)]
                       + scratch
                       + [pltpu.VMEM((HEADS_PER_STEP, nb, LANES, w), _F32)],
        compiler_params=pltpu.CompilerParams(
            dimension_semantics=("arbitrary", "arbitrary"),
            vmem_limit_bytes=VMEM_LIMIT_BYTES),
        name=name,
    )(q_t, k, v_t, *extra_inputs)


def _moba_attention(q_t, k, v_t, seq, bias_tiles, cfar):
    w = MOBA_BLOCK
    nb = seq // w
    topk = min(MOBA_TOPK, nb)
    extra_specs = [
        pl.BlockSpec((HEADS_PER_STEP, 2, w, w), lambda hp, bi: (hp, 0, 0, 0)),
        pl.BlockSpec((HEADS_PER_STEP, 8, w), lambda hp, bi: (hp, 0, 0)),
    ]
    return _attention_call(functools.partial(_moba_kernel, topk=topk), q_t, k, v_t, seq,
                           [bias_tiles, cfar], extra_specs,
                           [pltpu.VMEM((HEADS_PER_STEP, nb, LANES, w), _BF16)],
                           "moba_attention")


def _sb_kernel(q_ref, k_ref, v_ref, o_ref, qT_ref, r_ref, oT_ref, *, nb):
    w = MOBA_BLOCK
    _store_head_queries(q_ref, qT_ref, nb)

    key = lax.broadcasted_iota(jnp.int32, (w, w), 0)
    col = lax.broadcasted_iota(jnp.int32, (w, w), 1)
    causal = key < col
    upper = jnp.where(col > key, 1.0, 0.0).astype(_BF16)

    def run_tiles(tiles, resume):
        logits, partial, state, r_final = {}, {}, {}, {}
        first_key = {}
        for h, i, j in tiles:
            first_key.setdefault((h, i), j)
        last_key = {(h, i): j for h, i, j in tiles}

        def issue_logits(t):
            h, i, j = tiles[t]
            logits[t] = jnp.dot(k_ref[j * w:(j + 1) * w, :], qT_ref[h, i],
                                preferred_element_type=_F32)

        def issue_cumsum(t):
            h, i, j = tiles[t]
            z = logits.pop(t)
            neg_abs = pltpu.bitcast(pltpu.bitcast(z, jnp.uint32) | jnp.uint32(0x80000000), _F32)
            sp = jnp.maximum(z, 0.0) + jnp.log(1.0 + jnp.exp2(neg_abs)) * LOG2E
            if j == i:
                sp = jnp.where(causal, sp, 0.0)
            after = jnp.dot(upper, sp.astype(_BF16), preferred_element_type=_F32)
            partial[t] = (z - sp, after, sp[0:1, :])

        def weigh_values(t):
            h, i, j = tiles[t]
            log_sig, after, sp_first = partial.pop(t)
            a = jnp.exp2(log_sig - after)
            if j == i:
                a = jnp.where(causal, a, 0.0)
            pv = jnp.dot(v_ref[:, j * w:(j + 1) * w], a.astype(_BF16),
                         preferred_element_type=_F32)
            tile_total = after[0:1, :] + sp_first
            if j == i:
                acc, r = pv, tile_total
            else:
                if j == first_key[h, i]:
                    acc_old, r_old = oT_ref[h, i], r_ref[h, i]
                else:
                    acc_old, r_old = state[h]
                acc = acc_old + pv * jnp.exp2(-r_old)
                r = r_old + tile_total
            state[h] = (acc, r)
            if j == last_key[h, i]:
                oT_ref[h, i] = acc
                r_final[h, i] = r
                if not resume:
                    r_ref[h, i] = r

        _software_pipeline(len(tiles), [issue_logits, issue_cumsum, weigh_values], SB_LOOKAHEAD)
        return r_final

    r_near = run_tiles(_causal_tiles(nb, newest=SB_NEAR_TILES), resume=False)
    far_tiles = _causal_tiles(nb, newest=-SB_NEAR_TILES)
    if far_tiles:
        r_min = None
        for h, i in sorted({(h, i) for h, i, _ in far_tiles}):
            r_min = r_near[h, i] if r_min is None else jnp.minimum(r_min, r_near[h, i])

        @pl.when(jnp.min(r_min) <= SB_SKIP_LOG2)
        def _():
            run_tiles(far_tiles, resume=True)

    _merge_and_store(o_ref, oT_ref, nb)


def _sb_attention(q_t, k, v_t, seq):
    nb = seq // MOBA_BLOCK
    return _attention_call(_sb_kernel, q_t, k, v_t, seq, [], [],
                           [pltpu.VMEM((HEADS_PER_STEP, nb, 1, MOBA_BLOCK), _F32)],
                           "stick_breaking_attention")


def kernel(x, rel_bias, w_qkv, w_o, ln_mix_g, ln_mix_b, w_up, w_down, ln_ffn_g, ln_ffn_b):
    b, s, d = x.shape
    depth = w_qkv.shape[0]
    alpha = (2.0 * depth) ** 0.25
    assert d % (HEADS_PER_STEP * HEAD_DIM) == 0
    bias_tiles, cfar = _bias_tiles(rel_bias)
    h = x.reshape(b * s, d)
    for i in range(depth):
        q_t, k, v_t = _qkv_proj(h, w_qkv[i])
        if i % 2 == 0:
            o = _moba_attention(q_t, k, v_t, s, bias_tiles, cfar)
        else:
            o = _sb_attention(q_t, k, v_t, s)
        h = _oproj_ln(o, h, w_o[i].astype(_BF16), ln_mix_g[i], ln_mix_b[i], alpha)
        h = _mlp_ln(h, w_up[i].astype(_BF16), w_down[i].astype(_BF16),
                    ln_ffn_g[i], ln_ffn_b[i], alpha)
    return h.reshape(b, s, d)
```

```python
import functools
import math

import numpy as np
import jax
import jax.numpy as jnp
from jax import lax
from jax.experimental import pallas as pl
from jax.experimental.pallas import tpu as pltpu

HEAD_DIM = 64
HEADS_PER_STEP = 2
LANES = 128
MOBA_BLOCK = 256
MOBA_TOPK = 3
REL_BUCKETS = 32
REL_MAX_DIST = 128
LN_EPS = 1e-5
TOKEN_TILE = 512
EPILOGUE_TOKEN_TILE = 1024
EPILOGUE_ROW_SUBTILE = 256
FF_CHUNK = 1024
VMEM_LIMIT_BYTES = 56 * 1024 * 1024
MOBA_LOOKAHEAD = 6
SB_LOOKAHEAD = 2
SB_NEAR_TILES = 2
SB_SKIP_LOG2 = 152.0
LOG2E = math.log2(math.e)
EXP2_SAFE_MAX = 126.0
BF16_ROW_TILE = 16

_F32 = jnp.float32
_BF16 = jnp.bfloat16
_NEG_INF = float("-inf")


def _t5_bucket_upper_bounds():
    n = np.arange(0, 4 * REL_MAX_DIST)
    max_exact = REL_BUCKETS // 2
    nf = np.maximum(n, 1).astype(np.float64)
    large = max_exact + (np.log(nf / max_exact) / math.log(REL_MAX_DIST / max_exact)
                         * (REL_BUCKETS - max_exact)).astype(np.int64)
    large = np.minimum(large, REL_BUCKETS - 1)
    bucket = np.where(n < max_exact, n, large)
    assert bucket[-1] == REL_BUCKETS - 1 and np.all(np.diff(bucket) >= 0)
    return tuple(int(n[bucket == b].max()) for b in range(REL_BUCKETS - 1))


_BUCKET_UB = _t5_bucket_upper_bounds()
assert _BUCKET_UB[-1] < MOBA_BLOCK


def _layer_norm(y, g, b):
    mu = jnp.mean(y, axis=-1, keepdims=True)
    yc = y - mu
    var = jnp.mean(yc * yc, axis=-1, keepdims=True)
    return yc * lax.rsqrt(var + LN_EPS) * g + b


_CONTRACT_LAST = (((1,), (1,)), ((), ()))


def _qkv_kernel(x_ref, wqT_ref, wk_ref, wvT_ref, qT_ref, k_ref, vT_ref, *, q_scale):
    xb = x_ref[...].astype(_BF16)
    q_t = lax.dot_general(wqT_ref[...], xb, _CONTRACT_LAST, preferred_element_type=_F32)
    qT_ref[...] = (q_t * q_scale).astype(_BF16)
    k_ref[...] = jnp.dot(xb, wk_ref[...], preferred_element_type=_F32).astype(_BF16)
    v_t = lax.dot_general(wvT_ref[...], xb, _CONTRACT_LAST, preferred_element_type=_F32)
    vT_ref[...] = v_t.astype(_BF16)


def _qkv_proj(h2d, w_qkv):
    t, d = h2d.shape
    tm = min(TOKEN_TILE, t)
    assert t % tm == 0 and w_qkv.shape == (d, 3 * d)
    wq_t = w_qkv[:, :d].T.astype(_BF16)
    wk = w_qkv[:, d:2 * d].astype(_BF16)
    wv_t = w_qkv[:, 2 * d:].T.astype(_BF16)
    weight_spec = pl.BlockSpec((d, d), lambda i: (0, 0), pipeline_mode=pl.Buffered(1))
    transposed = jax.ShapeDtypeStruct((d, t), _BF16)
    return pl.pallas_call(
        functools.partial(_qkv_kernel, q_scale=LOG2E * HEAD_DIM ** -0.5),
        grid=(t // tm,),
        in_specs=[pl.BlockSpec((tm, d), lambda i: (i, 0)), weight_spec, weight_spec, weight_spec],
        out_specs=[pl.BlockSpec((d, tm), lambda i: (0, i)),
                   pl.BlockSpec((tm, d), lambda i: (i, 0)),
                   pl.BlockSpec((d, tm), lambda i: (0, i))],
        out_shape=[transposed, jax.ShapeDtypeStruct((t, d), _BF16), transposed],
        compiler_params=pltpu.CompilerParams(dimension_semantics=("arbitrary",),
                                             vmem_limit_bytes=VMEM_LIMIT_BYTES),
        name="qkv_proj",
    )(h2d, wq_t, wk, wv_t)


def _oproj_ln_kernel(o_ref, h_ref, w_ref, g_ref, b_ref, out_ref, *, alpha, sub):
    accs = {}

    def project(s):
        accs[s] = jnp.dot(o_ref[s * sub:(s + 1) * sub, :], w_ref[...],
                          preferred_element_type=_F32)

    def normalize(s):
        rows = slice(s * sub, (s + 1) * sub)
        out_ref[rows, :] = _layer_norm(alpha * h_ref[rows, :] + accs.pop(s),
                                       g_ref[...], b_ref[...])

    _software_pipeline(o_ref.shape[0] // sub, [project, normalize], 1)


def _oproj_ln(o2d, h2d, w_o_bf16, g, b, alpha):
    t, d = h2d.shape
    tm = min(EPILOGUE_TOKEN_TILE, t)
    sub = min(EPILOGUE_ROW_SUBTILE, tm)
    assert t % tm == 0 and tm % sub == 0
    return pl.pallas_call(
        functools.partial(_oproj_ln_kernel, alpha=alpha, sub=sub),
        grid=(t // tm,),
        in_specs=[pl.BlockSpec((tm, d), lambda i: (i, 0)),
                  pl.BlockSpec((tm, d), lambda i: (i, 0)),
                  pl.BlockSpec((d, d), lambda i: (0, 0), pipeline_mode=pl.Buffered(1)),
                  pl.BlockSpec((1, d), lambda i: (0, 0)),
                  pl.BlockSpec((1, d), lambda i: (0, 0))],
        out_specs=pl.BlockSpec((tm, d), lambda i: (i, 0)),
        out_shape=jax.ShapeDtypeStruct((t, d), _F32),
        compiler_params=pltpu.CompilerParams(dimension_semantics=("arbitrary",),
                                             vmem_limit_bytes=VMEM_LIMIT_BYTES),
        name="oproj_ln",
    )(o2d, h2d, w_o_bf16, g.reshape(1, d), b.reshape(1, d))


def _mlp_ln_kernel(h_ref, wu_ref, wd_ref, g_ref, b_ref, out_ref, *, alpha, d_ff, fc, sub):
    accs = {}

    def mix_channels(s):
        h = h_ref[s * sub:(s + 1) * sub, :]
        hb = h.astype(_BF16)
        acc = alpha * h
        for c in range(d_ff // fc):
            u = jnp.dot(hb, wu_ref[:, c * fc:(c + 1) * fc], preferred_element_type=_F32)
            u = jnp.square(jnp.maximum(u, 0.0)).astype(_BF16)
            acc = acc + jnp.dot(u, wd_ref[c * fc:(c + 1) * fc, :], preferred_element_type=_F32)
        accs[s] = acc

    def normalize(s):
        out_ref[s * sub:(s + 1) * sub, :] = _layer_norm(accs.pop(s), g_ref[...], b_ref[...])

    _software_pipeline(h_ref.shape[0] // sub, [mix_channels, normalize], 1)


def _mlp_ln(h2d, w_up_bf16, w_down_bf16, g, b, alpha):
    t, d = h2d.shape
    d_ff = w_up_bf16.shape[1]
    tm = min(EPILOGUE_TOKEN_TILE, t)
    sub = min(EPILOGUE_ROW_SUBTILE, tm)
    fc = min(FF_CHUNK, d_ff)
    assert t % tm == 0 and tm % sub == 0 and d_ff % fc == 0
    return pl.pallas_call(
        functools.partial(_mlp_ln_kernel, alpha=alpha, d_ff=d_ff, fc=fc, sub=sub),
        grid=(t // tm,),
        in_specs=[pl.BlockSpec((tm, d), lambda i: (i, 0)),
                  pl.BlockSpec((d, d_ff), lambda i: (0, 0), pipeline_mode=pl.Buffered(1)),
                  pl.BlockSpec((d_ff, d), lambda i: (0, 0), pipeline_mode=pl.Buffered(1)),
                  pl.BlockSpec((1, d), lambda i: (0, 0)),
                  pl.BlockSpec((1, d), lambda i: (0, 0))],
        out_specs=pl.BlockSpec((tm, d), lambda i: (i, 0)),
        out_shape=jax.ShapeDtypeStruct((t, d), _F32),
        compiler_params=pltpu.CompilerParams(dimension_semantics=("arbitrary",),
                                             vmem_limit_bytes=VMEM_LIMIT_BYTES),
        name="mlp_ln",
    )(h2d, w_up_bf16, w_down_bf16, g.reshape(1, d), b.reshape(1, d))


def _bias_kernel(tab_ref, bias_ref, cfar_ref):
    h = pl.program_id(0)
    w = MOBA_BLOCK
    key = lax.broadcasted_iota(jnp.int32, (w, w), 0)
    qry = lax.broadcasted_iota(jnp.int32, (w, w), 1)
    far = tab_ref[REL_BUCKETS - 1, h] * LOG2E
    for d in range(2):
        dist = qry - key + d * w
        val = jnp.full((w, w), far, _F32)
        for bkt in range(REL_BUCKETS - 2, -1, -1):
            val = jnp.where(dist <= _BUCKET_UB[bkt], tab_ref[bkt, h] * LOG2E, val)
        if d == 0:
            val = jnp.where(dist >= 0, val, _NEG_INF)
        bias_ref[0, d] = val
    cfar_ref[0] = jnp.full((8, w), far, _F32)


def _bias_tiles(rel_bias):
    n_heads = rel_bias.shape[1]
    w = MOBA_BLOCK
    return pl.pallas_call(
        _bias_kernel,
        grid=(n_heads,),
        in_specs=[pl.BlockSpec(memory_space=pltpu.SMEM)],
        out_specs=[pl.BlockSpec((1, 2, w, w), lambda h: (h, 0, 0, 0)),
                   pl.BlockSpec((1, 8, w), lambda h: (h, 0, 0))],
        out_shape=[jax.ShapeDtypeStruct((n_heads, 2, w, w), _F32),
                   jax.ShapeDtypeStruct((n_heads, 8, w), _F32)],
        compiler_params=pltpu.CompilerParams(dimension_semantics=("arbitrary",)),
        name="t5_bias_tiles",
    )(rel_bias.astype(_F32))


def _head_rows(h):
    row = lax.broadcasted_iota(jnp.int32, (LANES, MOBA_BLOCK), 0)
    return (row // HEAD_DIM) == h


def _store_head_queries(q_ref, qT_ref, nb):
    w = MOBA_BLOCK
    for h in range(HEADS_PER_STEP):
        mine = _head_rows(h)
        for i in range(nb):
            q_t = q_ref[:, i * w:(i + 1) * w]
            qT_ref[h, i] = jnp.where(mine, q_t, jnp.zeros_like(q_t))


def _head_slice(h):
    return slice(h * HEAD_DIM, (h + 1) * HEAD_DIM)


def _transpose_and_store(o_ref, oT_ref, nb):
    w = MOBA_BLOCK
    for i in range(nb):
        o_ref[i * w:(i + 1) * w, :] = oT_ref[i].T.astype(_BF16)


def _causal_tiles(nb, newest=None):
    tiles = []
    for i in range(nb):
        keys = list(range(i, -1, -1))
        if newest is not None:
            keys = keys[:newest] if newest > 0 else keys[-newest:]
        tiles += [(h, i, j) for j in keys for h in range(HEADS_PER_STEP)]
    return tiles


def _software_pipeline(n_items, stages, lookahead):
    n_stages = len(stages)
    for step in range(n_items + (n_stages - 1) * lookahead):
        for s, stage in enumerate(stages):
            t = step - s * lookahead
            if 0 <= t < n_items:
                stage(t)


def _moba_kernel(q_ref, k_ref, v_ref, bias_ref, cfar_ref, o_ref,
                 qT_ref, oT_ref, *, nb, topk):
    w = MOBA_BLOCK
    _store_head_queries(q_ref, qT_ref, nb)

    ones_rows = jnp.where(lax.broadcasted_iota(jnp.int32, (BF16_ROW_TILE, w), 0) == 0,
                          1.0, 0.0).astype(_BF16)

    def values_and_ones(h, j):
        return jnp.concatenate([v_ref[_head_slice(h), j * w:(j + 1) * w], ones_rows], axis=0)

    rows = [jnp.mean(k_ref[j * w:(j + 1) * w, :].astype(_F32), axis=0, keepdims=True)
            for j in range(nb)]
    kmean = jnp.concatenate(rows + [jnp.zeros((16 - nb, LANES), _F32)], axis=0)
    p0 = kmean.astype(_BF16)
    r1 = kmean - p0.astype(_F32)
    p1 = r1.astype(_BF16)
    p2 = (r1 - p1.astype(_F32)).astype(_BF16)
    kparts = jnp.concatenate([p0, p1, p2], axis=0)

    jidx = lax.broadcasted_iota(jnp.int32, (8, w), 0)
    addrows = {}
    for h in range(HEADS_PER_STEP):
        cfar = cfar_ref[h][0:1, :]
        for i in range(1, nb):
            g3 = jnp.dot(kparts, qT_ref[h, i], preferred_element_type=_F32)
            gate = g3[0:8] + g3[16:24] + g3[32:40]
            cnt = jnp.zeros((8, w), jnp.int32)
            for jp in range(i):
                rowv = gate[jp:jp + 1, :]
                beats = (rowv > gate) | ((rowv == gate) & (jp < jidx))
                cnt = cnt + jnp.where(beats, 1, 0)
            sel = (cnt < topk) & (jidx < i)
            addrows[h, i] = jnp.where(sel, jnp.where(jidx < i - 1, cfar, 0.0), _NEG_INF)

    tiles = _causal_tiles(nb)
    scores = {}
    state = {}

    def issue_scores(t):
        h, i, j = tiles[t]
        s = jnp.dot(k_ref[j * w:(j + 1) * w, :], qT_ref[h, i], preferred_element_type=_F32)
        if j >= i - 1:
            s = s + bias_ref[h, i - j]
        scores[t] = s

    def softmax_accumulate(t):
        h, i, j = tiles[t]
        s = scores.pop(t)
        mt = jnp.max(s, axis=0, keepdims=True)
        if j == i:
            m = mt
            p = jnp.exp2(s - m)
            acc = jnp.dot(values_and_ones(h, j), p.astype(_BF16), preferred_element_type=_F32)
        else:
            arow = addrows[h, i][j:j + 1, :]
            m_old, acc_old = state[h]
            m = jnp.maximum(m_old, mt + arow)
            a = jnp.exp2(m_old - m)
            p = jnp.exp2(s - (m - arow))
            acc = a * acc_old + jnp.dot(values_and_ones(h, j), p.astype(_BF16),
                                        preferred_element_type=_F32)
        state[h] = (m, acc)
        if j == 0:
            denom = acc[HEAD_DIM:HEAD_DIM + 1, :]
            oT_ref[i, _head_slice(h), :] = acc[0:HEAD_DIM, :] * (1.0 / denom)

    _software_pipeline(len(tiles), [issue_scores, softmax_accumulate], MOBA_LOOKAHEAD)
    _transpose_and_store(o_ref, oT_ref, nb)


def _attention_call(kernel, q_t, k, v_t, seq, extra_inputs, extra_specs, scratch, name):
    d, t = q_t.shape
    w = MOBA_BLOCK
    assert seq % w == 0 and t % seq == 0 and d % LANES == 0
    nb = seq // w
    assert nb <= 8
    transposed_spec = pl.BlockSpec((LANES, seq), lambda hp, bi: (hp, bi))
    token_major_spec = pl.BlockSpec((seq, LANES), lambda hp, bi: (bi, hp))
    return pl.pallas_call(
        functools.partial(kernel, nb=nb),
        grid=(d // LANES, t // seq),
        in_specs=[transposed_spec, token_major_spec, transposed_spec] + extra_specs,
        out_specs=token_major_spec,
        out_shape=jax.ShapeDtypeStruct((t, d), _BF16),
        scratch_shapes=[pltpu.VMEM((HEADS_PER_STEP, nb, LANES, w), _BF16)]
                       + scratch
                       + [pltpu.VMEM((nb, LANES, w), _F32)],
        compiler_params=pltpu.CompilerParams(
            dimension_semantics=("arbitrary", "arbitrary"),
            vmem_limit_bytes=VMEM_LIMIT_BYTES),
        name=name,
    )(q_t, k, v_t, *extra_inputs)


def _moba_attention(q_t, k, v_t, seq, bias_tiles, cfar):
    w = MOBA_BLOCK
    nb = seq // w
    topk = min(MOBA_TOPK, nb)
    extra_specs = [
        pl.BlockSpec((HEADS_PER_STEP, 2, w, w), lambda hp, bi: (hp, 0, 0, 0)),
        pl.BlockSpec((HEADS_PER_STEP, 8, w), lambda hp, bi: (hp, 0, 0)),
    ]
    return _attention_call(functools.partial(_moba_kernel, topk=topk), q_t, k, v_t, seq,
                           [bias_tiles, cfar], extra_specs, [], "moba_attention")


def _sb_kernel(q_ref, k_ref, v_ref, o_ref, qT_ref, r_ref, oT_ref, *, nb):
    w = MOBA_BLOCK
    _store_head_queries(q_ref, qT_ref, nb)

    key = lax.broadcasted_iota(jnp.int32, (w, w), 0)
    col = lax.broadcasted_iota(jnp.int32, (w, w), 1)
    causal = key < col
    upper = jnp.where(col > key, 1.0, 0.0).astype(_BF16)

    def run_tiles(tiles, resume):
        logits, partial, state, r_final = {}, {}, {}, {}
        first_key = {}
        for h, i, j in tiles:
            first_key.setdefault((h, i), j)
        last_key = {(h, i): j for h, i, j in tiles}

        def issue_logits(t):
            h, i, j = tiles[t]
            logits[t] = jnp.dot(k_ref[j * w:(j + 1) * w, :], qT_ref[h, i],
                                preferred_element_type=_F32)

        def issue_cumsum(t):
            h, i, j = tiles[t]
            z = logits.pop(t)
            sp = jnp.maximum(z, jnp.log(1.0 + jnp.exp2(jnp.minimum(z, EXP2_SAFE_MAX))) * LOG2E)
            if j == i:
                sp = jnp.where(causal, sp, 0.0)
            after = jnp.dot(upper, sp.astype(_BF16), preferred_element_type=_F32)
            partial[t] = (z - sp, after, sp[0:1, :])

        def weigh_values(t):
            h, i, j = tiles[t]
            log_sig, after, sp_first = partial.pop(t)
            a = jnp.exp2(log_sig - after)
            if j == i:
                a = jnp.where(causal, a, 0.0)
            pv = jnp.dot(v_ref[_head_slice(h), j * w:(j + 1) * w], a.astype(_BF16),
                         preferred_element_type=_F32)
            tile_total = after[0:1, :] + sp_first
            if j == i:
                acc, r = pv, tile_total
            else:
                if j == first_key[h, i]:
                    acc_old, r_old = oT_ref[i, _head_slice(h), :], r_ref[h, i]
                else:
                    acc_old, r_old = state[h]
                acc = acc_old + pv * jnp.exp2(-r_old)
                r = r_old + tile_total
            state[h] = (acc, r)
            if j == last_key[h, i]:
                oT_ref[i, _head_slice(h), :] = acc
                r_final[h, i] = r
                if not resume:
                    r_ref[h, i] = r

        _software_pipeline(len(tiles), [issue_logits, issue_cumsum, weigh_values], SB_LOOKAHEAD)
        return r_final

    r_near = run_tiles(_causal_tiles(nb, newest=SB_NEAR_TILES), resume=False)
    far_tiles = _causal_tiles(nb, newest=-SB_NEAR_TILES)
    if far_tiles:
        r_min = None
        for h, i in sorted({(h, i) for h, i, _ in far_tiles}):
            r_min = r_near[h, i] if r_min is None else jnp.minimum(r_min, r_near[h, i])

        @pl.when(jnp.min(r_min) <= SB_SKIP_LOG2)
        def _():
            run_tiles(far_tiles, resume=True)

    _transpose_and_store(o_ref, oT_ref, nb)


def _sb_attention(q_t, k, v_t, seq):
    nb = seq // MOBA_BLOCK
    return _attention_call(_sb_kernel, q_t, k, v_t, seq, [], [],
                           [pltpu.VMEM((HEADS_PER_STEP, nb, 1, MOBA_BLOCK), _F32)],
                           "stick_breaking_attention")


def kernel(x, rel_bias, w_qkv, w_o, ln_mix_g, ln_mix_b, w_up, w_down, ln_ffn_g, ln_ffn_b):
    b, s, d = x.shape
    depth = w_qkv.shape[0]
    alpha = (2.0 * depth) ** 0.25
    assert d % (HEADS_PER_STEP * HEAD_DIM) == 0
    bias_tiles, cfar = _bias_tiles(rel_bias)
    h = x.reshape(b * s, d)
    for i in range(depth):
        q_t, k, v_t = _qkv_proj(h, w_qkv[i])
        if i % 2 == 0:
            o = _moba_attention(q_t, k, v_t, s, bias_tiles, cfar)
        else:
            o = _sb_attention(q_t, k, v_t, s)
        h = _oproj_ln(o, h, w_o[i].astype(_BF16), ln_mix_g[i], ln_mix_b[i], alpha)
        h = _mlp_ln(h, w_up[i].astype(_BF16), w_down[i].astype(_BF16),
                    ln_ffn_g[i], ln_ffn_b[i], alpha)
    return h.reshape(b, s, d)
```

```python
import functools
import math

import numpy as np
import jax
import jax.numpy as jnp
from jax import lax
from jax.experimental import pallas as pl
from jax.experimental.pallas import tpu as pltpu

HEAD_DIM = 64
HEADS_PER_STEP = 2
LANES = 128
MOBA_BLOCK = 256
MOBA_TOPK = 3
REL_BUCKETS = 32
REL_MAX_DIST = 128
LN_EPS = 1e-5
TOKEN_TILE = 1024
EPILOGUE_TOKEN_TILE = 1024
EPILOGUE_ROW_SUBTILE = 256
FF_CHUNK = 1024
VMEM_LIMIT_BYTES = 56 * 1024 * 1024
MOBA_LOOKAHEAD = 6
SB_LOOKAHEAD = 2
SB_NEAR_TILES = 2
SB_SKIP_LOG2 = 152.0
LOG2E = math.log2(math.e)
EXP2_SAFE_MAX = 126.0
BF16_ROW_TILE = 16

_F32 = jnp.float32
_BF16 = jnp.bfloat16
_NEG_INF = float("-inf")


def _t5_bucket_upper_bounds():
    n = np.arange(0, 4 * REL_MAX_DIST)
    max_exact = REL_BUCKETS // 2
    nf = np.maximum(n, 1).astype(np.float64)
    large = max_exact + (np.log(nf / max_exact) / math.log(REL_MAX_DIST / max_exact)
                         * (REL_BUCKETS - max_exact)).astype(np.int64)
    large = np.minimum(large, REL_BUCKETS - 1)
    bucket = np.where(n < max_exact, n, large)
    assert bucket[-1] == REL_BUCKETS - 1 and np.all(np.diff(bucket) >= 0)
    return tuple(int(n[bucket == b].max()) for b in range(REL_BUCKETS - 1))


_BUCKET_UB = _t5_bucket_upper_bounds()
assert _BUCKET_UB[-1] < MOBA_BLOCK


def _layer_norm(y, g, b):
    mu = jnp.mean(y, axis=-1, keepdims=True)
    yc = y - mu
    var = jnp.mean(yc * yc, axis=-1, keepdims=True)
    return yc * lax.rsqrt(var + LN_EPS) * g + b


_CONTRACT_LAST = (((1,), (1,)), ((), ()))


def _qkv_kernel(x_ref, wqT_ref, wk_ref, wvT_ref, qT_ref, k_ref, vT_ref, *, q_scale):
    xb = x_ref[...].astype(_BF16)
    q_t = lax.dot_general(wqT_ref[...], xb, _CONTRACT_LAST, preferred_element_type=_F32)
    qT_ref[...] = (q_t * q_scale).astype(_BF16)
    k_ref[...] = jnp.dot(xb, wk_ref[...], preferred_element_type=_F32).astype(_BF16)
    v_t = lax.dot_general(wvT_ref[...], xb, _CONTRACT_LAST, preferred_element_type=_F32)
    vT_ref[...] = v_t.astype(_BF16)


def _qkv_proj(h2d, w_qkv):
    t, d = h2d.shape
    tm = min(TOKEN_TILE, t)
    assert t % tm == 0 and w_qkv.shape == (d, 3 * d)
    wq_t = w_qkv[:, :d].T.astype(_BF16)
    wk = w_qkv[:, d:2 * d].astype(_BF16)
    wv_t = w_qkv[:, 2 * d:].T.astype(_BF16)
    weight_spec = pl.BlockSpec((d, d), lambda i: (0, 0), pipeline_mode=pl.Buffered(1))
    transposed = jax.ShapeDtypeStruct((d, t), _BF16)
    return pl.pallas_call(
        functools.partial(_qkv_kernel, q_scale=LOG2E * HEAD_DIM ** -0.5),
        grid=(t // tm,),
        in_specs=[pl.BlockSpec((tm, d), lambda i: (i, 0)), weight_spec, weight_spec, weight_spec],
        out_specs=[pl.BlockSpec((d, tm), lambda i: (0, i)),
                   pl.BlockSpec((tm, d), lambda i: (i, 0)),
                   pl.BlockSpec((d, tm), lambda i: (0, i))],
        out_shape=[transposed, jax.ShapeDtypeStruct((t, d), _BF16), transposed],
        compiler_params=pltpu.CompilerParams(dimension_semantics=("arbitrary",),
                                             vmem_limit_bytes=VMEM_LIMIT_BYTES),
        name="qkv_proj",
    )(h2d, wq_t, wk, wv_t)


def _oproj_ln_kernel(o_ref, h_ref, w_ref, g_ref, b_ref, out_ref, *, alpha, sub):
    accs = {}

    def project(s):
        accs[s] = jnp.dot(o_ref[s * sub:(s + 1) * sub, :], w_ref[...],
                          preferred_element_type=_F32)

    def normalize(s):
        rows = slice(s * sub, (s + 1) * sub)
        out_ref[rows, :] = _layer_norm(alpha * h_ref[rows, :] + accs.pop(s),
                                       g_ref[...], b_ref[...])

    _software_pipeline(o_ref.shape[0] // sub, [project, normalize], 1)


def _oproj_ln(o2d, h2d, w_o_bf16, g, b, alpha):
    t, d = h2d.shape
    tm = min(EPILOGUE_TOKEN_TILE, t)
    sub = min(EPILOGUE_ROW_SUBTILE, tm)
    assert t % tm == 0 and tm % sub == 0
    return pl.pallas_call(
        functools.partial(_oproj_ln_kernel, alpha=alpha, sub=sub),
        grid=(t // tm,),
        in_specs=[pl.BlockSpec((tm, d), lambda i: (i, 0)),
                  pl.BlockSpec((tm, d), lambda i: (i, 0)),
                  pl.BlockSpec((d, d), lambda i: (0, 0), pipeline_mode=pl.Buffered(1)),
                  pl.BlockSpec((1, d), lambda i: (0, 0)),
                  pl.BlockSpec((1, d), lambda i: (0, 0))],
        out_specs=pl.BlockSpec((tm, d), lambda i: (i, 0)),
        out_shape=jax.ShapeDtypeStruct((t, d), _F32),
        compiler_params=pltpu.CompilerParams(dimension_semantics=("arbitrary",),
                                             vmem_limit_bytes=VMEM_LIMIT_BYTES),
        name="oproj_ln",
    )(o2d, h2d, w_o_bf16, g.reshape(1, d), b.reshape(1, d))


def _mlp_ln_kernel(h_ref, wu_ref, wd_ref, g_ref, b_ref, out_ref, *, alpha, d_ff, fc, sub):
    accs = {}

    def mix_channels(s):
        h = h_ref[s * sub:(s + 1) * sub, :]
        hb = h.astype(_BF16)
        acc = alpha * h
        for c in range(d_ff // fc):
            u = jnp.dot(hb, wu_ref[:, c * fc:(c + 1) * fc], preferred_element_type=_F32)
            u = jnp.square(jnp.maximum(u, 0.0)).astype(_BF16)
            acc = acc + jnp.dot(u, wd_ref[c * fc:(c + 1) * fc, :], preferred_element_type=_F32)
        accs[s] = acc

    def normalize(s):
        out_ref[s * sub:(s + 1) * sub, :] = _layer_norm(accs.pop(s), g_ref[...], b_ref[...])

    _software_pipeline(h_ref.shape[0] // sub, [mix_channels, normalize], 1)


def _mlp_ln(h2d, w_up_bf16, w_down_bf16, g, b, alpha):
    t, d = h2d.shape
    d_ff = w_up_bf16.shape[1]
    tm = min(EPILOGUE_TOKEN_TILE, t)
    sub = min(EPILOGUE_ROW_SUBTILE, tm)
    fc = min(FF_CHUNK, d_ff)
    assert t % tm == 0 and tm % sub == 0 and d_ff % fc == 0
    return pl.pallas_call(
        functools.partial(_mlp_ln_kernel, alpha=alpha, d_ff=d_ff, fc=fc, sub=sub),
        grid=(t // tm,),
        in_specs=[pl.BlockSpec((tm, d), lambda i: (i, 0)),
                  pl.BlockSpec((d, d_ff), lambda i: (0, 0), pipeline_mode=pl.Buffered(1)),
                  pl.BlockSpec((d_ff, d), lambda i: (0, 0), pipeline_mode=pl.Buffered(1)),
                  pl.BlockSpec((1, d), lambda i: (0, 0)),
                  pl.BlockSpec((1, d), lambda i: (0, 0))],
        out_specs=pl.BlockSpec((tm, d), lambda i: (i, 0)),
        out_shape=jax.ShapeDtypeStruct((t, d), _F32),
        compiler_params=pltpu.CompilerParams(dimension_semantics=("arbitrary",),
                                             vmem_limit_bytes=VMEM_LIMIT_BYTES),
        name="mlp_ln",
    )(h2d, w_up_bf16, w_down_bf16, g.reshape(1, d), b.reshape(1, d))


def _bias_kernel(tab_ref, bias_ref, cfar_ref):
    h = pl.program_id(0)
    w = MOBA_BLOCK
    key = lax.broadcasted_iota(jnp.int32, (w, w), 0)
    qry = lax.broadcasted_iota(jnp.int32, (w, w), 1)
    far = tab_ref[REL_BUCKETS - 1, h] * LOG2E
    for d in range(2):
        dist = qry - key + d * w
        val = jnp.full((w, w), far, _F32)
        for bkt in range(REL_BUCKETS - 2, -1, -1):
            val = jnp.where(dist <= _BUCKET_UB[bkt], tab_ref[bkt, h] * LOG2E, val)
        if d == 0:
            val = jnp.where(dist >= 0, val, _NEG_INF)
        bias_ref[0, d] = val
    cfar_ref[0] = jnp.full((8, w), far, _F32)


def _bias_tiles(rel_bias):
    n_heads = rel_bias.shape[1]
    w = MOBA_BLOCK
    return pl.pallas_call(
        _bias_kernel,
        grid=(n_heads,),
        in_specs=[pl.BlockSpec(memory_space=pltpu.SMEM)],
        out_specs=[pl.BlockSpec((1, 2, w, w), lambda h: (h, 0, 0, 0)),
                   pl.BlockSpec((1, 8, w), lambda h: (h, 0, 0))],
        out_shape=[jax.ShapeDtypeStruct((n_heads, 2, w, w), _F32),
                   jax.ShapeDtypeStruct((n_heads, 8, w), _F32)],
        compiler_params=pltpu.CompilerParams(dimension_semantics=("arbitrary",)),
        name="t5_bias_tiles",
    )(rel_bias.astype(_F32))


def _head_rows(h):
    row = lax.broadcasted_iota(jnp.int32, (LANES, MOBA_BLOCK), 0)
    return (row // HEAD_DIM) == h


def _store_head_queries(q_ref, qT_ref, nb):
    w = MOBA_BLOCK
    for h in range(HEADS_PER_STEP):
        mine = _head_rows(h)
        for i in range(nb):
            q_t = q_ref[:, i * w:(i + 1) * w]
            qT_ref[h, i] = jnp.where(mine, q_t, jnp.zeros_like(q_t))


def _head_slice(h):
    return slice(h * HEAD_DIM, (h + 1) * HEAD_DIM)


def _transpose_and_store(o_ref, oT_ref, nb):
    w = MOBA_BLOCK
    for i in range(nb):
        o_ref[i * w:(i + 1) * w, :] = oT_ref[i].T.astype(_BF16)


def _causal_tiles(nb, newest=None):
    tiles = []
    for i in range(nb):
        keys = list(range(i, -1, -1))
        if newest is not None:
            keys = keys[:newest] if newest > 0 else keys[-newest:]
        tiles += [(h, i, j) for j in keys for h in range(HEADS_PER_STEP)]
    return tiles


def _software_pipeline(n_items, stages, lookahead):
    n_stages = len(stages)
    for step in range(n_items + (n_stages - 1) * lookahead):
        for s, stage in enumerate(stages):
            t = step - s * lookahead
            if 0 <= t < n_items:
                stage(t)


def _moba_kernel(q_ref, k_ref, v_ref, bias_ref, cfar_ref, o_ref,
                 qT_ref, oT_ref, *, nb, topk):
    w = MOBA_BLOCK
    _store_head_queries(q_ref, qT_ref, nb)

    ones_rows = jnp.where(lax.broadcasted_iota(jnp.int32, (BF16_ROW_TILE, w), 0) == 0,
                          1.0, 0.0).astype(_BF16)

    def values_and_ones(h, j):
        return jnp.concatenate([v_ref[_head_slice(h), j * w:(j + 1) * w], ones_rows], axis=0)

    rows = [jnp.mean(k_ref[j * w:(j + 1) * w, :].astype(_F32), axis=0, keepdims=True)
            for j in range(nb)]
    kmean = jnp.concatenate(rows + [jnp.zeros((16 - nb, LANES), _F32)], axis=0)
    p0 = kmean.astype(_BF16)
    r1 = kmean - p0.astype(_F32)
    p1 = r1.astype(_BF16)
    p2 = (r1 - p1.astype(_F32)).astype(_BF16)
    kparts = jnp.concatenate([p0, p1, p2], axis=0)

    jidx = lax.broadcasted_iota(jnp.int32, (8, w), 0)
    addrows = {}
    for h in range(HEADS_PER_STEP):
        cfar = cfar_ref[h][0:1, :]
        for i in range(1, nb):
            g3 = jnp.dot(kparts, qT_ref[h, i], preferred_element_type=_F32)
            gate = g3[0:8] + g3[16:24] + g3[32:40]
            cnt = jnp.zeros((8, w), jnp.int32)
            for jp in range(i):
                rowv = gate[jp:jp + 1, :]
                beats = (rowv > gate) | ((rowv == gate) & (jp < jidx))
                cnt = cnt + jnp.where(beats, 1, 0)
            sel = (cnt < topk) & (jidx < i)
            addrows[h, i] = jnp.where(sel, jnp.where(jidx < i - 1, cfar, 0.0), _NEG_INF)

    tiles = _causal_tiles(nb)
    scores = {}
    state = {}

    def issue_scores(t):
        h, i, j = tiles[t]
        s = jnp.dot(k_ref[j * w:(j + 1) * w, :], qT_ref[h, i], preferred_element_type=_F32)
        if j >= i - 1:
            s = s + bias_ref[h, i - j]
        scores[t] = s

    def softmax_accumulate(t):
        h, i, j = tiles[t]
        s = scores.pop(t)
        mt = jnp.max(s, axis=0, keepdims=True)
        if j == i:
            m = mt
            p = jnp.exp2(s - m)
            acc = jnp.dot(values_and_ones(h, j), p.astype(_BF16), preferred_element_type=_F32)
        else:
            arow = addrows[h, i][j:j + 1, :]
            m_old, acc_old = state[h]
            m = jnp.maximum(m_old, mt + arow)
            a = jnp.exp2(m_old - m)
            p = jnp.exp2(s - (m - arow))
            acc = a * acc_old + jnp.dot(values_and_ones(h, j), p.astype(_BF16),
                                        preferred_element_type=_F32)
        state[h] = (m, acc)
        if j == 0:
            denom = acc[HEAD_DIM:HEAD_DIM + 1, :]
            oT_ref[i, _head_slice(h), :] = acc[0:HEAD_DIM, :] * (1.0 / denom)

    _software_pipeline(len(tiles), [issue_scores, softmax_accumulate], MOBA_LOOKAHEAD)
    _transpose_and_store(o_ref, oT_ref, nb)


def _attention_call(kernel, q_t, k, v_t, seq, extra_inputs, extra_specs, scratch, name):
    d, t = q_t.shape
    w = MOBA_BLOCK
    assert seq % w == 0 and t % seq == 0 and d % LANES == 0
    nb = seq // w
    assert nb <= 8
    transposed_spec = pl.BlockSpec((LANES, seq), lambda hp, bi: (hp, bi))
    token_major_spec = pl.BlockSpec((seq, LANES), lambda hp, bi: (bi, hp))
    return pl.pallas_call(
        functools.partial(kernel, nb=nb),
        grid=(d // LANES, t // seq),
        in_specs=[transposed_spec, token_major_spec, transposed_spec] + extra_specs,
        out_specs=token_major_spec,
        out_shape=jax.ShapeDtypeStruct((t, d), _BF16),
        scratch_shapes=[pltpu.VMEM((HEADS_PER_STEP, nb, LANES, w), _BF16)]
                       + scratch
                       + [pltpu.VMEM((nb, LANES, w), _F32)],
        compiler_params=pltpu.CompilerParams(
            dimension_semantics=("arbitrary", "arbitrary"),
            vmem_limit_bytes=VMEM_LIMIT_BYTES),
        name=name,
    )(q_t, k, v_t, *extra_inputs)


def _moba_attention(q_t, k, v_t, seq, bias_tiles, cfar):
    w = MOBA_BLOCK
    nb = seq // w
    topk = min(MOBA_TOPK, nb)
    extra_specs = [
        pl.BlockSpec((HEADS_PER_STEP, 2, w, w), lambda hp, bi: (hp, 0, 0, 0)),
        pl.BlockSpec((HEADS_PER_STEP, 8, w), lambda hp, bi: (hp, 0, 0)),
    ]
    return _attention_call(functools.partial(_moba_kernel, topk=topk), q_t, k, v_t, seq,
                           [bias_tiles, cfar], extra_specs, [], "moba_attention")


def _sb_kernel(q_ref, k_ref, v_ref, o_ref, qT_ref, r_ref, oT_ref, *, nb):
    w = MOBA_BLOCK
    _store_head_queries(q_ref, qT_ref, nb)

    key = lax.broadcasted_iota(jnp.int32, (w, w), 0)
    col = lax.broadcasted_iota(jnp.int32, (w, w), 1)
    causal = key < col
    upper = jnp.where(col > key, 1.0, 0.0).astype(_BF16)

    def run_tiles(tiles, resume):
        logits, partial, state, r_final = {}, {}, {}, {}
        first_key = {}
        for h, i, j in tiles:
            first_key.setdefault((h, i), j)
        last_key = {(h, i): j for h, i, j in tiles}

        def issue_logits(t):
            h, i, j = tiles[t]
            logits[t] = jnp.dot(k_ref[j * w:(j + 1) * w, :], qT_ref[h, i],
                                preferred_element_type=_F32)

        def issue_cumsum(t):
            h, i, j = tiles[t]
            z = logits.pop(t)
            sp = jnp.maximum(z, jnp.log(1.0 + jnp.exp2(jnp.minimum(z, EXP2_SAFE_MAX))) * LOG2E)
            if j == i:
                sp = jnp.where(causal, sp, 0.0)
            after = jnp.dot(upper, sp.astype(_BF16), preferred_element_type=_F32)
            partial[t] = (z - sp, after, sp[0:1, :])

        def weigh_values(t):
            h, i, j = tiles[t]
            log_sig, after, sp_first = partial.pop(t)
            a = jnp.exp2(log_sig - after)
            if j == i:
                a = jnp.where(causal, a, 0.0)
            pv = jnp.dot(v_ref[_head_slice(h), j * w:(j + 1) * w], a.astype(_BF16),
                         preferred_element_type=_F32)
            tile_total = after[0:1, :] + sp_first
            if j == i:
                acc, r = pv, tile_total
            else:
                if j == first_key[h, i]:
                    acc_old, r_old = oT_ref[i, _head_slice(h), :], r_ref[h, i]
                else:
                    acc_old, r_old = state[h]
                acc = acc_old + pv * jnp.exp2(-r_old)
                r = r_old + tile_total
            state[h] = (acc, r)
            if j == last_key[h, i]:
                oT_ref[i, _head_slice(h), :] = acc
                r_final[h, i] = r
                if not resume:
                    r_ref[h, i] = r

        _software_pipeline(len(tiles), [issue_logits, issue_cumsum, weigh_values], SB_LOOKAHEAD)
        return r_final

    r_near = run_tiles(_causal_tiles(nb, newest=SB_NEAR_TILES), resume=False)
    far_tiles = _causal_tiles(nb, newest=-SB_NEAR_TILES)
    if far_tiles:
        r_min = None
        for h, i in sorted({(h, i) for h, i, _ in far_tiles}):
            r_min = r_near[h, i] if r_min is None else jnp.minimum(r_min, r_near[h, i])

        @pl.when(jnp.min(r_min) <= SB_SKIP_LOG2)
        def _():
            run_tiles(far_tiles, resume=True)

    _transpose_and_store(o_ref, oT_ref, nb)


def _sb_attention(q_t, k, v_t, seq):
    nb = seq // MOBA_BLOCK
    return _attention_call(_sb_kernel, q_t, k, v_t, seq, [], [],
                           [pltpu.VMEM((HEADS_PER_STEP, nb, 1, MOBA_BLOCK), _F32)],
                           "stick_breaking_attention")


def kernel(x, rel_bias, w_qkv, w_o, ln_mix_g, ln_mix_b, w_up, w_down, ln_ffn_g, ln_ffn_b):
    b, s, d = x.shape
    depth = w_qkv.shape[0]
    alpha = (2.0 * depth) ** 0.25
    assert d % (HEADS_PER_STEP * HEAD_DIM) == 0
    bias_tiles, cfar = _bias_tiles(rel_bias)
    h = x.reshape(b * s, d)
    for i in range(depth):
        q_t, k, v_t = _qkv_proj(h, w_qkv[i])
        if i % 2 == 0:
            o = _moba_attention(q_t, k, v_t, s, bias_tiles, cfar)
        else:
            o = _sb_attention(q_t, k, v_t, s)
        h = _oproj_ln(o, h, w_o[i].astype(_BF16), ln_mix_g[i], ln_mix_b[i], alpha)
        h = _mlp_ln(h, w_up[i].astype(_BF16), w_down[i].astype(_BF16),
                    ln_ffn_g[i], ln_ffn_b[i], alpha)
    return h.reshape(b, s, d)
```

```python
import functools
import math

import numpy as np
import jax
import jax.numpy as jnp
from jax import lax
from jax.experimental import pallas as pl
from jax.experimental.pallas import tpu as pltpu

HEAD_DIM = 64
HEADS_PER_STEP = 2
LANES = 128
MOBA_BLOCK = 256
MOBA_TOPK = 3
REL_BUCKETS = 32
REL_MAX_DIST = 128
LN_EPS = 1e-5
TOKEN_TILE = 1024
EPILOGUE_TOKEN_TILE = 1024
EPILOGUE_ROW_SUBTILE = 256
FF_CHUNK = 1024
VMEM_LIMIT_BYTES = 56 * 1024 * 1024
MOBA_LOOKAHEAD = 6
SB_LOOKAHEAD = 2
SB_NEAR_TILES = 2
SB_SKIP_LOG2 = 152.0
LOG2E = math.log2(math.e)
EXP2_SAFE_MAX = 126.0
BF16_ROW_TILE = 16

_F32 = jnp.float32
_BF16 = jnp.bfloat16
_NEG_INF = float("-inf")


def _t5_bucket_upper_bounds():
    n = np.arange(0, 4 * REL_MAX_DIST)
    max_exact = REL_BUCKETS // 2
    nf = np.maximum(n, 1).astype(np.float64)
    large = max_exact + (np.log(nf / max_exact) / math.log(REL_MAX_DIST / max_exact)
                         * (REL_BUCKETS - max_exact)).astype(np.int64)
    large = np.minimum(large, REL_BUCKETS - 1)
    bucket = np.where(n < max_exact, n, large)
    assert bucket[-1] == REL_BUCKETS - 1 and np.all(np.diff(bucket) >= 0)
    return tuple(int(n[bucket == b].max()) for b in range(REL_BUCKETS - 1))


_BUCKET_UB = _t5_bucket_upper_bounds()
assert _BUCKET_UB[-1] < MOBA_BLOCK


def _layer_norm(y, g, b):
    mu = jnp.mean(y, axis=-1, keepdims=True)
    yc = y - mu
    var = jnp.mean(yc * yc, axis=-1, keepdims=True)
    return yc * lax.rsqrt(var + LN_EPS) * g + b


_CONTRACT_LAST = (((1,), (1,)), ((), ()))


def _qkv_kernel(x_ref, wqT_ref, wk_ref, wvT_ref, qT_ref, k_ref, vT_ref, *, q_scale):
    xb = x_ref[...].astype(_BF16)
    q_t = lax.dot_general(wqT_ref[...], xb, _CONTRACT_LAST, preferred_element_type=_F32)
    qT_ref[...] = (q_t * q_scale).astype(_BF16)
    k_ref[...] = jnp.dot(xb, wk_ref[...], preferred_element_type=_F32).astype(_BF16)
    v_t = lax.dot_general(wvT_ref[...], xb, _CONTRACT_LAST, preferred_element_type=_F32)
    vT_ref[...] = v_t.astype(_BF16)


def _qkv_proj(h2d, w_qkv):
    t, d = h2d.shape
    tm = min(TOKEN_TILE, t)
    assert t % tm == 0 and w_qkv.shape == (d, 3 * d)
    wq_t = w_qkv[:, :d].T.astype(_BF16)
    wk = w_qkv[:, d:2 * d].astype(_BF16)
    wv_t = w_qkv[:, 2 * d:].T.astype(_BF16)
    weight_spec = pl.BlockSpec((d, d), lambda i: (0, 0), pipeline_mode=pl.Buffered(1))
    transposed = jax.ShapeDtypeStruct((d, t), _BF16)
    return pl.pallas_call(
        functools.partial(_qkv_kernel, q_scale=LOG2E * HEAD_DIM ** -0.5),
        grid=(t // tm,),
        in_specs=[pl.BlockSpec((tm, d), lambda i: (i, 0)), weight_spec, weight_spec, weight_spec],
        out_specs=[pl.BlockSpec((d, tm), lambda i: (0, i)),
                   pl.BlockSpec((tm, d), lambda i: (i, 0)),
                   pl.BlockSpec((d, tm), lambda i: (0, i))],
        out_shape=[transposed, jax.ShapeDtypeStruct((t, d), _BF16), transposed],
        compiler_params=pltpu.CompilerParams(dimension_semantics=("arbitrary",),
                                             vmem_limit_bytes=VMEM_LIMIT_BYTES),
        name="qkv_proj",
    )(h2d, wq_t, wk, wv_t)


def _oproj_ln_kernel(o_ref, h_ref, w_ref, g_ref, b_ref, out_ref, *, alpha, sub):
    accs = {}

    def project(s):
        accs[s] = jnp.dot(o_ref[s * sub:(s + 1) * sub, :], w_ref[...],
                          preferred_element_type=_F32)

    def normalize(s):
        rows = slice(s * sub, (s + 1) * sub)
        out_ref[rows, :] = _layer_norm(alpha * h_ref[rows, :] + accs.pop(s),
                                       g_ref[...], b_ref[...])

    _software_pipeline(o_ref.shape[0] // sub, [project, normalize], 1)


def _oproj_ln(o2d, h2d, w_o_bf16, g, b, alpha):
    t, d = h2d.shape
    tm = min(EPILOGUE_TOKEN_TILE, t)
    sub = min(EPILOGUE_ROW_SUBTILE, tm)
    assert t % tm == 0 and tm % sub == 0
    return pl.pallas_call(
        functools.partial(_oproj_ln_kernel, alpha=alpha, sub=sub),
        grid=(t // tm,),
        in_specs=[pl.BlockSpec((tm, d), lambda i: (i, 0)),
                  pl.BlockSpec((tm, d), lambda i: (i, 0)),
                  pl.BlockSpec((d, d), lambda i: (0, 0), pipeline_mode=pl.Buffered(1)),
                  pl.BlockSpec((1, d), lambda i: (0, 0)),
                  pl.BlockSpec((1, d), lambda i: (0, 0))],
        out_specs=pl.BlockSpec((tm, d), lambda i: (i, 0)),
        out_shape=jax.ShapeDtypeStruct((t, d), _F32),
        compiler_params=pltpu.CompilerParams(dimension_semantics=("arbitrary",),
                                             vmem_limit_bytes=VMEM_LIMIT_BYTES),
        name="oproj_ln",
    )(o2d, h2d, w_o_bf16, g.reshape(1, d), b.reshape(1, d))


def _mlp_ln_kernel(h_ref, wu_ref, wd_ref, g_ref, b_ref, out_ref, *, alpha, d_ff, fc, sub):
    accs = {}

    def mix_channels(s):
        h = h_ref[s * sub:(s + 1) * sub, :]
        hb = h.astype(_BF16)
        acc = alpha * h
        for c in range(d_ff // fc):
            u = jnp.dot(hb, wu_ref[:, c * fc:(c + 1) * fc], preferred_element_type=_F32)
            u = jnp.square(jnp.maximum(u, 0.0)).astype(_BF16)
            acc = acc + jnp.dot(u, wd_ref[c * fc:(c + 1) * fc, :], preferred_element_type=_F32)
        accs[s] = acc

    def normalize(s):
        out_ref[s * sub:(s + 1) * sub, :] = _layer_norm(accs.pop(s), g_ref[...], b_ref[...])

    _software_pipeline(h_ref.shape[0] // sub, [mix_channels, normalize], 1)


def _mlp_ln(h2d, w_up_bf16, w_down_bf16, g, b, alpha):
    t, d = h2d.shape
    d_ff = w_up_bf16.shape[1]
    tm = min(EPILOGUE_TOKEN_TILE, t)
    sub = min(EPILOGUE_ROW_SUBTILE, tm)
    fc = min(FF_CHUNK, d_ff)
    assert t % tm == 0 and tm % sub == 0 and d_ff % fc == 0
    return pl.pallas_call(
        functools.partial(_mlp_ln_kernel, alpha=alpha, d_ff=d_ff, fc=fc, sub=sub),
        grid=(t // tm,),
        in_specs=[pl.BlockSpec((tm, d), lambda i: (i, 0)),
                  pl.BlockSpec((d, d_ff), lambda i: (0, 0), pipeline_mode=pl.Buffered(1)),
                  pl.BlockSpec((d_ff, d), lambda i: (0, 0), pipeline_mode=pl.Buffered(1)),
                  pl.BlockSpec((1, d), lambda i: (0, 0)),
                  pl.BlockSpec((1, d), lambda i: (0, 0))],
        out_specs=pl.BlockSpec((tm, d), lambda i: (i, 0)),
        out_shape=jax.ShapeDtypeStruct((t, d), _F32),
        compiler_params=pltpu.CompilerParams(dimension_semantics=("arbitrary",),
                                             vmem_limit_bytes=VMEM_LIMIT_BYTES),
        name="mlp_ln",
    )(h2d, w_up_bf16, w_down_bf16, g.reshape(1, d), b.reshape(1, d))


def _bias_kernel(tab_ref, bias_ref, cfar_ref):
    h = pl.program_id(0)
    w = MOBA_BLOCK
    key = lax.broadcasted_iota(jnp.int32, (w, w), 0)
    qry = lax.broadcasted_iota(jnp.int32, (w, w), 1)
    far = tab_ref[REL_BUCKETS - 1, h] * LOG2E
    for d in range(2):
        dist = qry - key + d * w
        val = jnp.full((w, w), far, _F32)
        for bkt in range(REL_BUCKETS - 2, -1, -1):
            val = jnp.where(dist <= _BUCKET_UB[bkt], tab_ref[bkt, h] * LOG2E, val)
        if d == 0:
            val = jnp.where(dist >= 0, val, _NEG_INF)
        bias_ref[0, d] = val
    cfar_ref[0] = jnp.full((8, w), far, _F32)


def _bias_tiles(rel_bias):
    n_heads = rel_bias.shape[1]
    w = MOBA_BLOCK
    return pl.pallas_call(
        _bias_kernel,
        grid=(n_heads,),
        in_specs=[pl.BlockSpec(memory_space=pltpu.SMEM)],
        out_specs=[pl.BlockSpec((1, 2, w, w), lambda h: (h, 0, 0, 0)),
                   pl.BlockSpec((1, 8, w), lambda h: (h, 0, 0))],
        out_shape=[jax.ShapeDtypeStruct((n_heads, 2, w, w), _F32),
                   jax.ShapeDtypeStruct((n_heads, 8, w), _F32)],
        compiler_params=pltpu.CompilerParams(dimension_semantics=("arbitrary",)),
        name="t5_bias_tiles",
    )(rel_bias.astype(_F32))


def _head_rows(h):
    row = lax.broadcasted_iota(jnp.int32, (LANES, MOBA_BLOCK), 0)
    return (row // HEAD_DIM) == h


def _store_head_queries(q_ref, qT_ref, nb):
    w = MOBA_BLOCK
    for h in range(HEADS_PER_STEP):
        mine = _head_rows(h)
        for i in range(nb):
            q_t = q_ref[:, i * w:(i + 1) * w]
            qT_ref[h, i] = jnp.where(mine, q_t, jnp.zeros_like(q_t))


def _head_slice(h):
    return slice(h * HEAD_DIM, (h + 1) * HEAD_DIM)


def _store_query_block(o_ref, oT_ref, h, i, result):
    w = MOBA_BLOCK
    oT_ref[i, _head_slice(h), :] = result
    if h == HEADS_PER_STEP - 1:
        o_ref[i * w:(i + 1) * w, :] = oT_ref[i].T.astype(_BF16)


def _causal_tiles(nb, newest=None):
    tiles = []
    for i in range(nb):
        keys = list(range(i, -1, -1))
        if newest is not None:
            keys = keys[:newest] if newest > 0 else keys[-newest:]
        tiles += [(h, i, j) for j in keys for h in range(HEADS_PER_STEP)]
    return tiles


def _software_pipeline(n_items, stages, lookahead):
    n_stages = len(stages)
    for step in range(n_items + (n_stages - 1) * lookahead):
        for s, stage in enumerate(stages):
            t = step - s * lookahead
            if 0 <= t < n_items:
                stage(t)


def _moba_kernel(q_ref, k_ref, v_ref, bias_ref, cfar_ref, o_ref,
                 qT_ref, oT_ref, *, nb, topk):
    w = MOBA_BLOCK
    _store_head_queries(q_ref, qT_ref, nb)

    ones_rows = jnp.where(lax.broadcasted_iota(jnp.int32, (BF16_ROW_TILE, w), 0) == 0,
                          1.0, 0.0).astype(_BF16)

    def values_and_ones(h, j):
        return jnp.concatenate([v_ref[_head_slice(h), j * w:(j + 1) * w], ones_rows], axis=0)

    rows = [jnp.mean(k_ref[j * w:(j + 1) * w, :].astype(_F32), axis=0, keepdims=True)
            for j in range(nb)]
    kmean = jnp.concatenate(rows + [jnp.zeros((16 - nb, LANES), _F32)], axis=0)
    p0 = kmean.astype(_BF16)
    r1 = kmean - p0.astype(_F32)
    p1 = r1.astype(_BF16)
    p2 = (r1 - p1.astype(_F32)).astype(_BF16)
    kparts = jnp.concatenate([p0, p1, p2], axis=0)

    jidx = lax.broadcasted_iota(jnp.int32, (8, w), 0)
    addrows = {}
    for h in range(HEADS_PER_STEP):
        cfar = cfar_ref[h][0:1, :]
        for i in range(1, nb):
            g3 = jnp.dot(kparts, qT_ref[h, i], preferred_element_type=_F32)
            gate = g3[0:8] + g3[16:24] + g3[32:40]
            cnt = jnp.zeros((8, w), jnp.int32)
            for jp in range(i):
                rowv = gate[jp:jp + 1, :]
                beats = (rowv > gate) | ((rowv == gate) & (jp < jidx))
                cnt = cnt + jnp.where(beats, 1, 0)
            sel = (cnt < topk) & (jidx < i)
            addrows[h, i] = jnp.where(sel, jnp.where(jidx < i - 1, cfar, 0.0), _NEG_INF)

    tiles = _causal_tiles(nb)
    scores = {}
    state = {}

    def issue_scores(t):
        h, i, j = tiles[t]
        s = jnp.dot(k_ref[j * w:(j + 1) * w, :], qT_ref[h, i], preferred_element_type=_F32)
        if j >= i - 1:
            s = s + bias_ref[h, i - j]
        scores[t] = s

    def softmax_accumulate(t):
        h, i, j = tiles[t]
        s = scores.pop(t)
        mt = jnp.max(s, axis=0, keepdims=True)
        if j == i:
            m = mt
            p = jnp.exp2(s - m)
            acc = jnp.dot(values_and_ones(h, j), p.astype(_BF16), preferred_element_type=_F32)
        else:
            arow = addrows[h, i][j:j + 1, :]
            m_old, acc_old = state[h]
            m = jnp.maximum(m_old, mt + arow)
            a = jnp.exp2(m_old - m)
            p = jnp.exp2(s - (m - arow))
            acc = a * acc_old + jnp.dot(values_and_ones(h, j), p.astype(_BF16),
                                        preferred_element_type=_F32)
        state[h] = (m, acc)
        if j == 0:
            denom = acc[HEAD_DIM:HEAD_DIM + 1, :]
            _store_query_block(o_ref, oT_ref, h, i, acc[0:HEAD_DIM, :] * (1.0 / denom))

    _software_pipeline(len(tiles), [issue_scores, softmax_accumulate], MOBA_LOOKAHEAD)


def _attention_call(kernel, q_t, k, v_t, seq, extra_inputs, extra_specs, scratch, name):
    d, t = q_t.shape
    w = MOBA_BLOCK
    assert seq % w == 0 and t % seq == 0 and d % LANES == 0
    nb = seq // w
    assert nb <= 8
    transposed_spec = pl.BlockSpec((LANES, seq), lambda hp, bi: (hp, bi))
    token_major_spec = pl.BlockSpec((seq, LANES), lambda hp, bi: (bi, hp))
    return pl.pallas_call(
        functools.partial(kernel, nb=nb),
        grid=(d // LANES, t // seq),
        in_specs=[transposed_spec, token_major_spec, transposed_spec] + extra_specs,
        out_specs=token_major_spec,
        out_shape=jax.ShapeDtypeStruct((t, d), _BF16),
        scratch_shapes=[pltpu.VMEM((HEADS_PER_STEP, nb, LANES, w), _BF16)]
                       + scratch
                       + [pltpu.VMEM((nb, LANES, w), _F32)],
        compiler_params=pltpu.CompilerParams(
            dimension_semantics=("arbitrary", "arbitrary"),
            vmem_limit_bytes=VMEM_LIMIT_BYTES),
        name=name,
    )(q_t, k, v_t, *extra_inputs)


def _moba_attention(q_t, k, v_t, seq, bias_tiles, cfar):
    w = MOBA_BLOCK
    nb = seq // w
    topk = min(MOBA_TOPK, nb)
    extra_specs = [
        pl.BlockSpec((HEADS_PER_STEP, 2, w, w), lambda hp, bi: (hp, 0, 0, 0)),
        pl.BlockSpec((HEADS_PER_STEP, 8, w), lambda hp, bi: (hp, 0, 0)),
    ]
    return _attention_call(functools.partial(_moba_kernel, topk=topk), q_t, k, v_t, seq,
                           [bias_tiles, cfar], extra_specs, [], "moba_attention")


def _sb_kernel(q_ref, k_ref, v_ref, o_ref, qT_ref, r_ref, oT_ref, *, nb):
    w = MOBA_BLOCK
    _store_head_queries(q_ref, qT_ref, nb)

    key = lax.broadcasted_iota(jnp.int32, (w, w), 0)
    col = lax.broadcasted_iota(jnp.int32, (w, w), 1)
    causal = key < col
    upper = jnp.where(col > key, 1.0, 0.0).astype(_BF16)

    def run_tiles(tiles, resume):
        logits, partial, state, r_final = {}, {}, {}, {}
        first_key = {}
        for h, i, j in tiles:
            first_key.setdefault((h, i), j)
        last_key = {(h, i): j for h, i, j in tiles}

        def issue_logits(t):
            h, i, j = tiles[t]
            logits[t] = jnp.dot(k_ref[j * w:(j + 1) * w, :], qT_ref[h, i],
                                preferred_element_type=_F32)

        def issue_cumsum(t):
            h, i, j = tiles[t]
            z = logits.pop(t)
            sp = jnp.maximum(z, jnp.log(1.0 + jnp.exp2(jnp.minimum(z, EXP2_SAFE_MAX))) * LOG2E)
            if j == i:
                sp = jnp.where(causal, sp, 0.0)
            after = jnp.dot(upper, sp.astype(_BF16), preferred_element_type=_F32)
            partial[t] = (z - sp, after, sp[0:1, :])

        def weigh_values(t):
            h, i, j = tiles[t]
            log_sig, after, sp_first = partial.pop(t)
            a = jnp.exp2(log_sig - after)
            if j == i:
                a = jnp.where(causal, a, 0.0)
            pv = jnp.dot(v_ref[_head_slice(h), j * w:(j + 1) * w], a.astype(_BF16),
                         preferred_element_type=_F32)
            tile_total = after[0:1, :] + sp_first
            if j == i:
                acc, r = pv, tile_total
            else:
                if j == first_key[h, i]:
                    acc_old, r_old = oT_ref[i, _head_slice(h), :], r_ref[h, i]
                else:
                    acc_old, r_old = state[h]
                acc = acc_old + pv * jnp.exp2(-r_old)
                r = r_old + tile_total
            state[h] = (acc, r)
            if j == last_key[h, i]:
                _store_query_block(o_ref, oT_ref, h, i, acc)
                r_final[h, i] = r
                if not resume:
                    r_ref[h, i] = r

        _software_pipeline(len(tiles), [issue_logits, issue_cumsum, weigh_values], SB_LOOKAHEAD)
        return r_final

    r_near = run_tiles(_causal_tiles(nb, newest=SB_NEAR_TILES), resume=False)
    far_tiles = _causal_tiles(nb, newest=-SB_NEAR_TILES)
    if far_tiles:
        r_min = None
        for h, i in sorted({(h, i) for h, i, _ in far_tiles}):
            r_min = r_near[h, i] if r_min is None else jnp.minimum(r_min, r_near[h, i])

        @pl.when(jnp.min(r_min) <= SB_SKIP_LOG2)
        def _():
            run_tiles(far_tiles, resume=True)


def _sb_attention(q_t, k, v_t, seq):
    nb = seq // MOBA_BLOCK
    return _attention_call(_sb_kernel, q_t, k, v_t, seq, [], [],
                           [pltpu.VMEM((HEADS_PER_STEP, nb, 1, MOBA_BLOCK), _F32)],
                           "stick_breaking_attention")


def kernel(x, rel_bias, w_qkv, w_o, ln_mix_g, ln_mix_b, w_up, w_down, ln_ffn_g, ln_ffn_b):
    b, s, d = x.shape
    depth = w_qkv.shape[0]
    alpha = (2.0 * depth) ** 0.25
    assert d % (HEADS_PER_STEP * HEAD_DIM) == 0
    bias_tiles, cfar = _bias_tiles(rel_bias)
    h = x.reshape(b * s, d)
    for i in range(depth):
        q_t, k, v_t = _qkv_proj(h, w_qkv[i])
        if i % 2 == 0:
            o = _moba_attention(q_t, k, v_t, s, bias_tiles, cfar)
        else:
            o = _sb_attention(q_t, k, v_t, s)
        h = _oproj_ln(o, h, w_o[i].astype(_BF16), ln_mix_g[i], ln_mix_b[i], alpha)
        h = _mlp_ln(h, w_up[i].astype(_BF16), w_down[i].astype(_BF16),
                    ln_ffn_g[i], ln_ffn_b[i], alpha)
    return h.reshape(b, s, d)
```

```python
import functools
import math

import numpy as np
import jax
import jax.numpy as jnp
from jax import lax
from jax.experimental import pallas as pl
from jax.experimental.pallas import tpu as pltpu

HEAD_DIM = 64
HEADS_PER_STEP = 2
LANES = 128
MOBA_BLOCK = 256
MOBA_TOPK = 3
REL_BUCKETS = 32
REL_MAX_DIST = 128
LN_EPS = 1e-5
TOKEN_TILE = 1024
EPILOGUE_TOKEN_TILE = 1024
EPILOGUE_ROW_SUBTILE = 256
FF_CHUNK = 1024
VMEM_LIMIT_BYTES = 56 * 1024 * 1024
MOBA_LOOKAHEAD = 6
SEQS_PER_STEP = 2
SB_LOOKAHEAD = 2
SB_NEAR_TILES = 2
SB_SKIP_LOG2 = 152.0
LOG2E = math.log2(math.e)
EXP2_SAFE_MAX = 126.0
BF16_ROW_TILE = 16

_F32 = jnp.float32
_BF16 = jnp.bfloat16
_NEG_INF = float("-inf")


def _t5_bucket_upper_bounds():
    n = np.arange(0, 4 * REL_MAX_DIST)
    max_exact = REL_BUCKETS // 2
    nf = np.maximum(n, 1).astype(np.float64)
    large = max_exact + (np.log(nf / max_exact) / math.log(REL_MAX_DIST / max_exact)
                         * (REL_BUCKETS - max_exact)).astype(np.int64)
    large = np.minimum(large, REL_BUCKETS - 1)
    bucket = np.where(n < max_exact, n, large)
    assert bucket[-1] == REL_BUCKETS - 1 and np.all(np.diff(bucket) >= 0)
    return tuple(int(n[bucket == b].max()) for b in range(REL_BUCKETS - 1))


_BUCKET_UB = _t5_bucket_upper_bounds()
assert _BUCKET_UB[-1] < MOBA_BLOCK


def _layer_norm(y, g, b):
    mu = jnp.mean(y, axis=-1, keepdims=True)
    yc = y - mu
    var = jnp.mean(yc * yc, axis=-1, keepdims=True)
    return yc * lax.rsqrt(var + LN_EPS) * g + b


_CONTRACT_LAST = (((1,), (1,)), ((), ()))


def _qkv_kernel(x_ref, wqT_ref, wk_ref, wvT_ref, qT_ref, k_ref, vT_ref, *, q_scale):
    xb = x_ref[...].astype(_BF16)
    q_t = lax.dot_general(wqT_ref[...], xb, _CONTRACT_LAST, preferred_element_type=_F32)
    qT_ref[...] = (q_t * q_scale).astype(_BF16)
    k_ref[...] = jnp.dot(xb, wk_ref[...], preferred_element_type=_F32).astype(_BF16)
    v_t = lax.dot_general(wvT_ref[...], xb, _CONTRACT_LAST, preferred_element_type=_F32)
    vT_ref[...] = v_t.astype(_BF16)


def _qkv_proj(h2d, w_qkv):
    t, d = h2d.shape
    tm = min(TOKEN_TILE, t)
    assert t % tm == 0 and w_qkv.shape == (d, 3 * d)
    wq_t = w_qkv[:, :d].T.astype(_BF16)
    wk = w_qkv[:, d:2 * d].astype(_BF16)
    wv_t = w_qkv[:, 2 * d:].T.astype(_BF16)
    weight_spec = pl.BlockSpec((d, d), lambda i: (0, 0), pipeline_mode=pl.Buffered(1))
    transposed = jax.ShapeDtypeStruct((d, t), _BF16)
    return pl.pallas_call(
        functools.partial(_qkv_kernel, q_scale=LOG2E * HEAD_DIM ** -0.5),
        grid=(t // tm,),
        in_specs=[pl.BlockSpec((tm, d), lambda i: (i, 0)), weight_spec, weight_spec, weight_spec],
        out_specs=[pl.BlockSpec((d, tm), lambda i: (0, i)),
                   pl.BlockSpec((tm, d), lambda i: (i, 0)),
                   pl.BlockSpec((d, tm), lambda i: (0, i))],
        out_shape=[transposed, jax.ShapeDtypeStruct((t, d), _BF16), transposed],
        compiler_params=pltpu.CompilerParams(dimension_semantics=("arbitrary",),
                                             vmem_limit_bytes=VMEM_LIMIT_BYTES),
        name="qkv_proj",
    )(h2d, wq_t, wk, wv_t)


def _oproj_ln_kernel(o_ref, h_ref, w_ref, g_ref, b_ref, out_ref, *, alpha, sub):
    accs = {}

    def project(s):
        accs[s] = jnp.dot(o_ref[s * sub:(s + 1) * sub, :], w_ref[...],
                          preferred_element_type=_F32)

    def normalize(s):
        rows = slice(s * sub, (s + 1) * sub)
        out_ref[rows, :] = _layer_norm(alpha * h_ref[rows, :] + accs.pop(s),
                                       g_ref[...], b_ref[...])

    _software_pipeline(o_ref.shape[0] // sub, [project, normalize], 1)


def _oproj_ln(o2d, h2d, w_o_bf16, g, b, alpha):
    t, d = h2d.shape
    tm = min(EPILOGUE_TOKEN_TILE, t)
    sub = min(EPILOGUE_ROW_SUBTILE, tm)
    assert t % tm == 0 and tm % sub == 0
    return pl.pallas_call(
        functools.partial(_oproj_ln_kernel, alpha=alpha, sub=sub),
        grid=(t // tm,),
        in_specs=[pl.BlockSpec((tm, d), lambda i: (i, 0)),
                  pl.BlockSpec((tm, d), lambda i: (i, 0)),
                  pl.BlockSpec((d, d), lambda i: (0, 0), pipeline_mode=pl.Buffered(1)),
                  pl.BlockSpec((1, d), lambda i: (0, 0)),
                  pl.BlockSpec((1, d), lambda i: (0, 0))],
        out_specs=pl.BlockSpec((tm, d), lambda i: (i, 0)),
        out_shape=jax.ShapeDtypeStruct((t, d), _F32),
        compiler_params=pltpu.CompilerParams(dimension_semantics=("arbitrary",),
                                             vmem_limit_bytes=VMEM_LIMIT_BYTES),
        name="oproj_ln",
    )(o2d, h2d, w_o_bf16, g.reshape(1, d), b.reshape(1, d))


def _mlp_ln_kernel(h_ref, wu_ref, wd_ref, g_ref, b_ref, out_ref, *, alpha, d_ff, fc, sub):
    accs = {}

    def mix_channels(s):
        h = h_ref[s * sub:(s + 1) * sub, :]
        hb = h.astype(_BF16)
        acc = alpha * h
        for c in range(d_ff // fc):
            u = jnp.dot(hb, wu_ref[:, c * fc:(c + 1) * fc], preferred_element_type=_F32)
            u = jnp.square(jnp.maximum(u, 0.0)).astype(_BF16)
            acc = acc + jnp.dot(u, wd_ref[c * fc:(c + 1) * fc, :], preferred_element_type=_F32)
        accs[s] = acc

    def normalize(s):
        out_ref[s * sub:(s + 1) * sub, :] = _layer_norm(accs.pop(s), g_ref[...], b_ref[...])

    _software_pipeline(h_ref.shape[0] // sub, [mix_channels, normalize], 1)


def _mlp_ln(h2d, w_up_bf16, w_down_bf16, g, b, alpha):
    t, d = h2d.shape
    d_ff = w_up_bf16.shape[1]
    tm = min(EPILOGUE_TOKEN_TILE, t)
    sub = min(EPILOGUE_ROW_SUBTILE, tm)
    fc = min(FF_CHUNK, d_ff)
    assert t % tm == 0 and tm % sub == 0 and d_ff % fc == 0
    return pl.pallas_call(
        functools.partial(_mlp_ln_kernel, alpha=alpha, d_ff=d_ff, fc=fc, sub=sub),
        grid=(t // tm,),
        in_specs=[pl.BlockSpec((tm, d), lambda i: (i, 0)),
                  pl.BlockSpec((d, d_ff), lambda i: (0, 0), pipeline_mode=pl.Buffered(1)),
                  pl.BlockSpec((d_ff, d), lambda i: (0, 0), pipeline_mode=pl.Buffered(1)),
                  pl.BlockSpec((1, d), lambda i: (0, 0)),
                  pl.BlockSpec((1, d), lambda i: (0, 0))],
        out_specs=pl.BlockSpec((tm, d), lambda i: (i, 0)),
        out_shape=jax.ShapeDtypeStruct((t, d), _F32),
        compiler_params=pltpu.CompilerParams(dimension_semantics=("arbitrary",),
                                             vmem_limit_bytes=VMEM_LIMIT_BYTES),
        name="mlp_ln",
    )(h2d, w_up_bf16, w_down_bf16, g.reshape(1, d), b.reshape(1, d))


def _bias_kernel(tab_ref, bias_ref, cfar_ref):
    h = pl.program_id(0)
    w = MOBA_BLOCK
    key = lax.broadcasted_iota(jnp.int32, (w, w), 0)
    qry = lax.broadcasted_iota(jnp.int32, (w, w), 1)
    far = tab_ref[REL_BUCKETS - 1, h] * LOG2E
    for d in range(2):
        dist = qry - key + d * w
        val = jnp.full((w, w), far, _F32)
        for bkt in range(REL_BUCKETS - 2, -1, -1):
            val = jnp.where(dist <= _BUCKET_UB[bkt], tab_ref[bkt, h] * LOG2E, val)
        if d == 0:
            val = jnp.where(dist >= 0, val, _NEG_INF)
        bias_ref[0, d] = val
    cfar_ref[0] = jnp.full((8, w), far, _F32)


def _bias_tiles(rel_bias):
    n_heads = rel_bias.shape[1]
    w = MOBA_BLOCK
    return pl.pallas_call(
        _bias_kernel,
        grid=(n_heads,),
        in_specs=[pl.BlockSpec(memory_space=pltpu.SMEM)],
        out_specs=[pl.BlockSpec((1, 2, w, w), lambda h: (h, 0, 0, 0)),
                   pl.BlockSpec((1, 8, w), lambda h: (h, 0, 0))],
        out_shape=[jax.ShapeDtypeStruct((n_heads, 2, w, w), _F32),
                   jax.ShapeDtypeStruct((n_heads, 8, w), _F32)],
        compiler_params=pltpu.CompilerParams(dimension_semantics=("arbitrary",)),
        name="t5_bias_tiles",
    )(rel_bias.astype(_F32))


def _head_rows(h):
    row = lax.broadcasted_iota(jnp.int32, (LANES, MOBA_BLOCK), 0)
    return (row // HEAD_DIM) == h


def _store_head_queries(q_ref, qT_ref, nb):
    w = MOBA_BLOCK
    for h in range(HEADS_PER_STEP):
        mine = _head_rows(h)
        for i in range(nb):
            q_t = q_ref[:, i * w:(i + 1) * w]
            qT_ref[h, i] = jnp.where(mine, q_t, jnp.zeros_like(q_t))


def _head_slice(h):
    return slice(h * HEAD_DIM, (h + 1) * HEAD_DIM)


def _store_query_block(o_ref, oT_ref, h, i, result):
    w = MOBA_BLOCK
    oT_ref[i, _head_slice(h), :] = result
    if h == HEADS_PER_STEP - 1:
        o_ref[i * w:(i + 1) * w, :] = oT_ref[i].T.astype(_BF16)


def _causal_tiles(nb, seqs, newest=None):
    tiles = []
    for base in range(0, seqs * nb, nb):
        for i in range(nb):
            keys = list(range(i, -1, -1))
            if newest is not None:
                keys = keys[:newest] if newest > 0 else keys[-newest:]
            tiles += [(h, base + i, base + j) for j in keys for h in range(HEADS_PER_STEP)]
    return tiles


def _software_pipeline(n_items, stages, lookahead):
    n_stages = len(stages)
    for step in range(n_items + (n_stages - 1) * lookahead):
        for s, stage in enumerate(stages):
            t = step - s * lookahead
            if 0 <= t < n_items:
                stage(t)


def _moba_kernel(q_ref, k_ref, v_ref, bias_ref, cfar_ref, o_ref,
                 qT_ref, oT_ref, *, nb, seqs, topk):
    w = MOBA_BLOCK
    _store_head_queries(q_ref, qT_ref, seqs * nb)

    ones_rows = jnp.where(lax.broadcasted_iota(jnp.int32, (BF16_ROW_TILE, w), 0) == 0,
                          1.0, 0.0).astype(_BF16)

    def values_and_ones(h, j):
        return jnp.concatenate([v_ref[_head_slice(h), j * w:(j + 1) * w], ones_rows], axis=0)

    jidx = lax.broadcasted_iota(jnp.int32, (8, w), 0)
    addrows = {}
    for base in range(0, seqs * nb, nb):
        rows = [jnp.mean(k_ref[(base + j) * w:(base + j + 1) * w, :].astype(_F32),
                         axis=0, keepdims=True) for j in range(nb)]
        kmean = jnp.concatenate(rows + [jnp.zeros((16 - nb, LANES), _F32)], axis=0)
        p0 = kmean.astype(_BF16)
        r1 = kmean - p0.astype(_F32)
        p1 = r1.astype(_BF16)
        p2 = (r1 - p1.astype(_F32)).astype(_BF16)
        kparts = jnp.concatenate([p0, p1, p2], axis=0)
        for h in range(HEADS_PER_STEP):
            cfar = cfar_ref[h][0:1, :]
            for i in range(1, nb):
                g3 = jnp.dot(kparts, qT_ref[h, base + i], preferred_element_type=_F32)
                gate = g3[0:8] + g3[16:24] + g3[32:40]
                cnt = jnp.zeros((8, w), jnp.int32)
                for jp in range(i):
                    rowv = gate[jp:jp + 1, :]
                    beats = (rowv > gate) | ((rowv == gate) & (jp < jidx))
                    cnt = cnt + jnp.where(beats, 1, 0)
                sel = (cnt < topk) & (jidx < i)
                addrows[h, base + i] = jnp.where(sel, jnp.where(jidx < i - 1, cfar, 0.0),
                                                 _NEG_INF)

    tiles = _causal_tiles(nb, seqs)
    scores = {}
    state = {}

    def issue_scores(t):
        h, i, j = tiles[t]
        s = jnp.dot(k_ref[j * w:(j + 1) * w, :], qT_ref[h, i], preferred_element_type=_F32)
        if j >= i - 1:
            s = s + bias_ref[h, i - j]
        scores[t] = s

    def softmax_accumulate(t):
        h, i, j = tiles[t]
        s = scores.pop(t)
        mt = jnp.max(s, axis=0, keepdims=True)
        if j == i:
            m = mt
            p = jnp.exp2(s - m)
            acc = jnp.dot(values_and_ones(h, j), p.astype(_BF16), preferred_element_type=_F32)
        else:
            arow = addrows[h, i][j % nb:j % nb + 1, :]
            m_old, acc_old = state[h]
            m = jnp.maximum(m_old, mt + arow)
            a = jnp.exp2(m_old - m)
            p = jnp.exp2(s - (m - arow))
            acc = a * acc_old + jnp.dot(values_and_ones(h, j), p.astype(_BF16),
                                        preferred_element_type=_F32)
        state[h] = (m, acc)
        if j % nb == 0:
            denom = acc[HEAD_DIM:HEAD_DIM + 1, :]
            _store_query_block(o_ref, oT_ref, h, i, acc[0:HEAD_DIM, :] * (1.0 / denom))

    _software_pipeline(len(tiles), [issue_scores, softmax_accumulate], MOBA_LOOKAHEAD)


def _attention_call(kernel, q_t, k, v_t, seq, extra_inputs, extra_specs, scratch, name):
    d, t = q_t.shape
    w = MOBA_BLOCK
    assert seq % w == 0 and t % seq == 0 and d % LANES == 0
    nb = seq // w
    assert nb <= 8
    seqs = SEQS_PER_STEP if (t // seq) % SEQS_PER_STEP == 0 else 1
    rows = seqs * seq
    transposed_spec = pl.BlockSpec((LANES, rows), lambda hp, bi: (hp, bi))
    token_major_spec = pl.BlockSpec((rows, LANES), lambda hp, bi: (bi, hp))
    return pl.pallas_call(
        functools.partial(kernel, nb=nb, seqs=seqs),
        grid=(d // LANES, t // rows),
        in_specs=[transposed_spec, token_major_spec, transposed_spec] + extra_specs,
        out_specs=token_major_spec,
        out_shape=jax.ShapeDtypeStruct((t, d), _BF16),
        scratch_shapes=[pltpu.VMEM((HEADS_PER_STEP, seqs * nb, LANES, w), _BF16)]
                       + [pltpu.VMEM(shape(seqs), dt) for shape, dt in scratch]
                       + [pltpu.VMEM((seqs * nb, LANES, w), _F32)],
        compiler_params=pltpu.CompilerParams(
            dimension_semantics=("arbitrary", "arbitrary"),
            vmem_limit_bytes=VMEM_LIMIT_BYTES),
        name=name,
    )(q_t, k, v_t, *extra_inputs)


def _moba_attention(q_t, k, v_t, seq, bias_tiles, cfar):
    w = MOBA_BLOCK
    nb = seq // w
    topk = min(MOBA_TOPK, nb)
    extra_specs = [
        pl.BlockSpec((HEADS_PER_STEP, 2, w, w), lambda hp, bi: (hp, 0, 0, 0)),
        pl.BlockSpec((HEADS_PER_STEP, 8, w), lambda hp, bi: (hp, 0, 0)),
    ]
    return _attention_call(functools.partial(_moba_kernel, topk=topk), q_t, k, v_t, seq,
                           [bias_tiles, cfar], extra_specs, [], "moba_attention")


def _sb_kernel(q_ref, k_ref, v_ref, o_ref, qT_ref, r_ref, oT_ref, *, nb, seqs):
    w = MOBA_BLOCK
    _store_head_queries(q_ref, qT_ref, seqs * nb)

    key = lax.broadcasted_iota(jnp.int32, (w, w), 0)
    col = lax.broadcasted_iota(jnp.int32, (w, w), 1)
    causal = key < col
    upper = jnp.where(col > key, 1.0, 0.0).astype(_BF16)

    def run_tiles(tiles, resume):
        logits, partial, state, r_final = {}, {}, {}, {}
        first_key = {}
        for h, i, j in tiles:
            first_key.setdefault((h, i), j)
        last_key = {(h, i): j for h, i, j in tiles}

        def issue_logits(t):
            h, i, j = tiles[t]
            logits[t] = jnp.dot(k_ref[j * w:(j + 1) * w, :], qT_ref[h, i],
                                preferred_element_type=_F32)

        def issue_cumsum(t):
            h, i, j = tiles[t]
            z = logits.pop(t)
            sp = jnp.maximum(z, jnp.log(1.0 + jnp.exp2(jnp.minimum(z, EXP2_SAFE_MAX))) * LOG2E)
            if j == i:
                sp = jnp.where(causal, sp, 0.0)
            after = jnp.dot(upper, sp.astype(_BF16), preferred_element_type=_F32)
            partial[t] = (z - sp, after, sp[0:1, :])

        def weigh_values(t):
            h, i, j = tiles[t]
            log_sig, after, sp_first = partial.pop(t)
            a = jnp.exp2(log_sig - after)
            if j == i:
                a = jnp.where(causal, a, 0.0)
            pv = jnp.dot(v_ref[_head_slice(h), j * w:(j + 1) * w], a.astype(_BF16),
                         preferred_element_type=_F32)
            tile_total = after[0:1, :] + sp_first
            if j == i:
                acc, r = pv, tile_total
            else:
                if j == first_key[h, i]:
                    acc_old, r_old = oT_ref[i, _head_slice(h), :], r_ref[h, i]
                else:
                    acc_old, r_old = state[h]
                acc = acc_old + pv * jnp.exp2(-r_old)
                r = r_old + tile_total
            state[h] = (acc, r)
            if j == last_key[h, i]:
                _store_query_block(o_ref, oT_ref, h, i, acc)
                r_final[h, i] = r
                if not resume:
                    r_ref[h, i] = r

        _software_pipeline(len(tiles), [issue_logits, issue_cumsum, weigh_values], SB_LOOKAHEAD)
        return r_final

    r_near = run_tiles(_causal_tiles(nb, seqs, newest=SB_NEAR_TILES), resume=False)
    far_tiles = _causal_tiles(nb, seqs, newest=-SB_NEAR_TILES)
    if far_tiles:
        r_min = None
        for h, i in sorted({(h, i) for h, i, _ in far_tiles}):
            r_min = r_near[h, i] if r_min is None else jnp.minimum(r_min, r_near[h, i])

        @pl.when(jnp.min(r_min) <= SB_SKIP_LOG2)
        def _():
            run_tiles(far_tiles, resume=True)


def _sb_attention(q_t, k, v_t, seq):
    nb = seq // MOBA_BLOCK
    r_scratch = (lambda seqs: (HEADS_PER_STEP, seqs * nb, 1, MOBA_BLOCK), _F32)
    return _attention_call(_sb_kernel, q_t, k, v_t, seq, [], [], [r_scratch],
                           "stick_breaking_attention")


def kernel(x, rel_bias, w_qkv, w_o, ln_mix_g, ln_mix_b, w_up, w_down, ln_ffn_g, ln_ffn_b):
    b, s, d = x.shape
    depth = w_qkv.shape[0]
    alpha = (2.0 * depth) ** 0.25
    assert d % (HEADS_PER_STEP * HEAD_DIM) == 0
    bias_tiles, cfar = _bias_tiles(rel_bias)
    h = x.reshape(b * s, d)
    for i in range(depth):
        q_t, k, v_t = _qkv_proj(h, w_qkv[i])
        if i % 2 == 0:
            o = _moba_attention(q_t, k, v_t, s, bias_tiles, cfar)
        else:
            o = _sb_attention(q_t, k, v_t, s)
        h = _oproj_ln(o, h, w_o[i].astype(_BF16), ln_mix_g[i], ln_mix_b[i], alpha)
        h = _mlp_ln(h, w_up[i].astype(_BF16), w_down[i].astype(_BF16),
                    ln_ffn_g[i], ln_ffn_b[i], alpha)
    return h.reshape(b, s, d)
```

```python
import functools
import math

import numpy as np
import jax
import jax.numpy as jnp
from jax import lax
from jax.experimental import pallas as pl
from jax.experimental.pallas import tpu as pltpu

HEAD_DIM = 64
HEADS_PER_STEP = 2
LANES = 128
MOBA_BLOCK = 256
MOBA_TOPK = 3
REL_BUCKETS = 32
REL_MAX_DIST = 128
LN_EPS = 1e-5
QKV_TOKEN_TILE = 1024
OPROJ_TOKEN_TILE = 2048
MLP_TOKEN_TILE = 1024
EPILOGUE_ROW_SUBTILE = 256
FF_CHUNK = 1024
VMEM_LIMIT_BYTES = 56 * 1024 * 1024
SEQS_PER_STEP = 2
MOBA_LOOKAHEAD = 6
SB_LOOKAHEAD = 2
SB_NEAR_TILES = 2
SB_SKIP_LOG2 = 152.0
LOG2E = math.log2(math.e)
EXP2_SAFE_MAX = 126.0
BF16_ROW_TILE = 16

_F32 = jnp.float32
_BF16 = jnp.bfloat16
_NEG_INF = float("-inf")


def _t5_bucket_upper_bounds():
    n = np.arange(0, 4 * REL_MAX_DIST)
    max_exact = REL_BUCKETS // 2
    nf = np.maximum(n, 1).astype(np.float64)
    large = max_exact + (np.log(nf / max_exact) / math.log(REL_MAX_DIST / max_exact)
                         * (REL_BUCKETS - max_exact)).astype(np.int64)
    large = np.minimum(large, REL_BUCKETS - 1)
    bucket = np.where(n < max_exact, n, large)
    assert bucket[-1] == REL_BUCKETS - 1 and np.all(np.diff(bucket) >= 0)
    return tuple(int(n[bucket == b].max()) for b in range(REL_BUCKETS - 1))


_BUCKET_UB = _t5_bucket_upper_bounds()
assert _BUCKET_UB[-1] < MOBA_BLOCK


def _layer_norm(y, g, b):
    mu = jnp.mean(y, axis=-1, keepdims=True)
    yc = y - mu
    var = jnp.mean(yc * yc, axis=-1, keepdims=True)
    return yc * lax.rsqrt(var + LN_EPS) * g + b


_CONTRACT_LAST = (((1,), (1,)), ((), ()))


def _qkv_kernel(x_ref, wqT_ref, wk_ref, wvT_ref, qT_ref, k_ref, vT_ref, *, q_scale):
    xb = x_ref[...].astype(_BF16)
    q_t = lax.dot_general(wqT_ref[...], xb, _CONTRACT_LAST, preferred_element_type=_F32)
    qT_ref[...] = (q_t * q_scale).astype(_BF16)
    k_ref[...] = jnp.dot(xb, wk_ref[...], preferred_element_type=_F32).astype(_BF16)
    v_t = lax.dot_general(wvT_ref[...], xb, _CONTRACT_LAST, preferred_element_type=_F32)
    vT_ref[...] = v_t.astype(_BF16)


def _qkv_proj(h2d, w_qkv):
    t, d = h2d.shape
    tm = min(QKV_TOKEN_TILE, t)
    assert t % tm == 0 and w_qkv.shape == (d, 3 * d)
    wq_t = w_qkv[:, :d].T.astype(_BF16)
    wk = w_qkv[:, d:2 * d].astype(_BF16)
    wv_t = w_qkv[:, 2 * d:].T.astype(_BF16)
    weight_spec = pl.BlockSpec((d, d), lambda i: (0, 0), pipeline_mode=pl.Buffered(1))
    transposed = jax.ShapeDtypeStruct((d, t), _BF16)
    return pl.pallas_call(
        functools.partial(_qkv_kernel, q_scale=LOG2E * HEAD_DIM ** -0.5),
        grid=(t // tm,),
        in_specs=[pl.BlockSpec((tm, d), lambda i: (i, 0)), weight_spec, weight_spec, weight_spec],
        out_specs=[pl.BlockSpec((d, tm), lambda i: (0, i)),
                   pl.BlockSpec((tm, d), lambda i: (i, 0)),
                   pl.BlockSpec((d, tm), lambda i: (0, i))],
        out_shape=[transposed, jax.ShapeDtypeStruct((t, d), _BF16), transposed],
        compiler_params=pltpu.CompilerParams(dimension_semantics=("arbitrary",),
                                             vmem_limit_bytes=VMEM_LIMIT_BYTES),
        name="qkv_proj",
    )(h2d, wq_t, wk, wv_t)


def _oproj_ln_kernel(o_ref, h_ref, w_ref, g_ref, b_ref, out_ref, *, alpha, sub):
    accs = {}

    def project(s):
        accs[s] = jnp.dot(o_ref[s * sub:(s + 1) * sub, :], w_ref[...],
                          preferred_element_type=_F32)

    def normalize(s):
        rows = slice(s * sub, (s + 1) * sub)
        out_ref[rows, :] = _layer_norm(alpha * h_ref[rows, :] + accs.pop(s),
                                       g_ref[...], b_ref[...])

    _software_pipeline(o_ref.shape[0] // sub, [project, normalize], 1)


def _oproj_ln(o2d, h2d, w_o_bf16, g, b, alpha):
    t, d = h2d.shape
    tm = min(OPROJ_TOKEN_TILE, t)
    sub = min(EPILOGUE_ROW_SUBTILE, tm)
    assert t % tm == 0 and tm % sub == 0
    return pl.pallas_call(
        functools.partial(_oproj_ln_kernel, alpha=alpha, sub=sub),
        grid=(t // tm,),
        in_specs=[pl.BlockSpec((tm, d), lambda i: (i, 0)),
                  pl.BlockSpec((tm, d), lambda i: (i, 0)),
                  pl.BlockSpec((d, d), lambda i: (0, 0), pipeline_mode=pl.Buffered(1)),
                  pl.BlockSpec((1, d), lambda i: (0, 0)),
                  pl.BlockSpec((1, d), lambda i: (0, 0))],
        out_specs=pl.BlockSpec((tm, d), lambda i: (i, 0)),
        out_shape=jax.ShapeDtypeStruct((t, d), _F32),
        compiler_params=pltpu.CompilerParams(dimension_semantics=("arbitrary",),
                                             vmem_limit_bytes=VMEM_LIMIT_BYTES),
        name="oproj_ln",
    )(o2d, h2d, w_o_bf16, g.reshape(1, d), b.reshape(1, d))


def _mlp_ln_kernel(h_ref, wu_ref, wd_ref, g_ref, b_ref, out_ref, *, alpha, d_ff, fc, sub):
    accs = {}

    def mix_channels(s):
        h = h_ref[s * sub:(s + 1) * sub, :]
        hb = h.astype(_BF16)
        acc = alpha * h
        for c in range(d_ff // fc):
            u = jnp.dot(hb, wu_ref[:, c * fc:(c + 1) * fc], preferred_element_type=_F32)
            u = jnp.square(jnp.maximum(u, 0.0)).astype(_BF16)
            acc = acc + jnp.dot(u, wd_ref[c * fc:(c + 1) * fc, :], preferred_element_type=_F32)
        accs[s] = acc

    def normalize(s):
        out_ref[s * sub:(s + 1) * sub, :] = _layer_norm(accs.pop(s), g_ref[...], b_ref[...])

    _software_pipeline(h_ref.shape[0] // sub, [mix_channels, normalize], 1)


def _mlp_ln(h2d, w_up_bf16, w_down_bf16, g, b, alpha):
    t, d = h2d.shape
    d_ff = w_up_bf16.shape[1]
    tm = min(MLP_TOKEN_TILE, t)
    sub = min(EPILOGUE_ROW_SUBTILE, tm)
    fc = min(FF_CHUNK, d_ff)
    assert t % tm == 0 and tm % sub == 0 and d_ff % fc == 0
    return pl.pallas_call(
        functools.partial(_mlp_ln_kernel, alpha=alpha, d_ff=d_ff, fc=fc, sub=sub),
        grid=(t // tm,),
        in_specs=[pl.BlockSpec((tm, d), lambda i: (i, 0)),
                  pl.BlockSpec((d, d_ff), lambda i: (0, 0), pipeline_mode=pl.Buffered(1)),
                  pl.BlockSpec((d_ff, d), lambda i: (0, 0), pipeline_mode=pl.Buffered(1)),
                  pl.BlockSpec((1, d), lambda i: (0, 0)),
                  pl.BlockSpec((1, d), lambda i: (0, 0))],
        out_specs=pl.BlockSpec((tm, d), lambda i: (i, 0)),
        out_shape=jax.ShapeDtypeStruct((t, d), _F32),
        compiler_params=pltpu.CompilerParams(dimension_semantics=("arbitrary",),
                                             vmem_limit_bytes=VMEM_LIMIT_BYTES),
        name="mlp_ln",
    )(h2d, w_up_bf16, w_down_bf16, g.reshape(1, d), b.reshape(1, d))


def _bias_kernel(tab_ref, bias_ref, cfar_ref):
    h = pl.program_id(0)
    w = MOBA_BLOCK
    key = lax.broadcasted_iota(jnp.int32, (w, w), 0)
    qry = lax.broadcasted_iota(jnp.int32, (w, w), 1)
    far = tab_ref[REL_BUCKETS - 1, h] * LOG2E
    for d in range(2):
        dist = qry - key + d * w
        val = jnp.full((w, w), far, _F32)
        for bkt in range(REL_BUCKETS - 2, -1, -1):
            val = jnp.where(dist <= _BUCKET_UB[bkt], tab_ref[bkt, h] * LOG2E, val)
        if d == 0:
            val = jnp.where(dist >= 0, val, _NEG_INF)
        bias_ref[0, d] = val
    cfar_ref[0] = jnp.full((8, w), far, _F32)


def _bias_tiles(rel_bias):
    n_heads = rel_bias.shape[1]
    w = MOBA_BLOCK
    return pl.pallas_call(
        _bias_kernel,
        grid=(n_heads,),
        in_specs=[pl.BlockSpec(memory_space=pltpu.SMEM)],
        out_specs=[pl.BlockSpec((1, 2, w, w), lambda h: (h, 0, 0, 0)),
                   pl.BlockSpec((1, 8, w), lambda h: (h, 0, 0))],
        out_shape=[jax.ShapeDtypeStruct((n_heads, 2, w, w), _F32),
                   jax.ShapeDtypeStruct((n_heads, 8, w), _F32)],
        compiler_params=pltpu.CompilerParams(dimension_semantics=("arbitrary",)),
        name="t5_bias_tiles",
    )(rel_bias.astype(_F32))


def _head_rows(h):
    row = lax.broadcasted_iota(jnp.int32, (LANES, MOBA_BLOCK), 0)
    return (row // HEAD_DIM) == h


def _store_head_queries(q_ref, qT_ref, nb):
    w = MOBA_BLOCK
    for h in range(HEADS_PER_STEP):
        mine = _head_rows(h)
        for i in range(nb):
            q_t = q_ref[:, i * w:(i + 1) * w]
            qT_ref[h, i] = jnp.where(mine, q_t, jnp.zeros_like(q_t))


def _head_slice(h):
    return slice(h * HEAD_DIM, (h + 1) * HEAD_DIM)


def _store_query_block(o_ref, oT_ref, h, i, result):
    w = MOBA_BLOCK
    oT_ref[i, _head_slice(h), :] = result
    if h == HEADS_PER_STEP - 1:
        o_ref[i * w:(i + 1) * w, :] = oT_ref[i].T.astype(_BF16)


def _causal_tiles(nb, seqs, newest=None):
    tiles = []
    for base in range(0, seqs * nb, nb):
        for i in range(nb):
            keys = list(range(i, -1, -1))
            if newest is not None:
                keys = keys[:newest] if newest > 0 else keys[-newest:]
            tiles += [(h, base + i, base + j) for j in keys for h in range(HEADS_PER_STEP)]
    return tiles


def _software_pipeline(n_items, stages, lookahead):
    n_stages = len(stages)
    for step in range(n_items + (n_stages - 1) * lookahead):
        for s, stage in enumerate(stages):
            t = step - s * lookahead
            if 0 <= t < n_items:
                stage(t)


def _moba_kernel(q_ref, k_ref, v_ref, bias_ref, cfar_ref, o_ref,
                 qT_ref, oT_ref, *, nb, seqs, topk):
    w = MOBA_BLOCK
    _store_head_queries(q_ref, qT_ref, seqs * nb)

    ones_rows = jnp.where(lax.broadcasted_iota(jnp.int32, (BF16_ROW_TILE, w), 0) == 0,
                          1.0, 0.0).astype(_BF16)

    def values_and_ones(h, j):
        return jnp.concatenate([v_ref[_head_slice(h), j * w:(j + 1) * w], ones_rows], axis=0)

    jidx = lax.broadcasted_iota(jnp.int32, (8, w), 0)
    addrows = {}
    for base in range(0, seqs * nb, nb):
        rows = [jnp.mean(k_ref[(base + j) * w:(base + j + 1) * w, :].astype(_F32),
                         axis=0, keepdims=True) for j in range(nb)]
        kmean = jnp.concatenate(rows + [jnp.zeros((16 - nb, LANES), _F32)], axis=0)
        p0 = kmean.astype(_BF16)
        r1 = kmean - p0.astype(_F32)
        p1 = r1.astype(_BF16)
        p2 = (r1 - p1.astype(_F32)).astype(_BF16)
        kparts = jnp.concatenate([p0, p1, p2], axis=0)
        for h in range(HEADS_PER_STEP):
            cfar = cfar_ref[h][0:1, :]
            for i in range(1, nb):
                g3 = jnp.dot(kparts, qT_ref[h, base + i], preferred_element_type=_F32)
                gate = g3[0:8] + g3[16:24] + g3[32:40]
                cnt = jnp.zeros((8, w), jnp.int32)
                for jp in range(i):
                    rowv = gate[jp:jp + 1, :]
                    beats = (rowv > gate) | ((rowv == gate) & (jp < jidx))
                    cnt = cnt + jnp.where(beats, 1, 0)
                sel = (cnt < topk) & (jidx < i)
                addrows[h, base + i] = jnp.where(sel, jnp.where(jidx < i - 1, cfar, 0.0),
                                                 _NEG_INF)

    tiles = _causal_tiles(nb, seqs)
    scores = {}
    state = {}

    def issue_scores(t):
        h, i, j = tiles[t]
        s = jnp.dot(k_ref[j * w:(j + 1) * w, :], qT_ref[h, i], preferred_element_type=_F32)
        if j >= i - 1:
            s = s + bias_ref[h, i - j]
        scores[t] = s

    def softmax_accumulate(t):
        h, i, j = tiles[t]
        s = scores.pop(t)
        mt = jnp.max(s, axis=0, keepdims=True)
        if j == i:
            m = mt
            p = jnp.exp2(s - m)
            acc = jnp.dot(values_and_ones(h, j), p.astype(_BF16), preferred_element_type=_F32)
        else:
            arow = addrows[h, i][j % nb:j % nb + 1, :]
            m_old, acc_old = state[h]
            m = jnp.maximum(m_old, mt + arow)
            a = jnp.exp2(m_old - m)
            p = jnp.exp2(s - (m - arow))
            acc = a * acc_old + jnp.dot(values_and_ones(h, j), p.astype(_BF16),
                                        preferred_element_type=_F32)
        state[h] = (m, acc)
        if j % nb == 0:
            denom = acc[HEAD_DIM:HEAD_DIM + 1, :]
            _store_query_block(o_ref, oT_ref, h, i, acc[0:HEAD_DIM, :] * (1.0 / denom))

    _software_pipeline(len(tiles), [issue_scores, softmax_accumulate], MOBA_LOOKAHEAD)


def _attention_call(kernel, q_t, k, v_t, seq, extra_inputs, extra_specs, scratch, name):
    d, t = q_t.shape
    w = MOBA_BLOCK
    assert seq % w == 0 and t % seq == 0 and d % LANES == 0
    nb = seq // w
    assert nb <= 8
    seqs = SEQS_PER_STEP if (t // seq) % SEQS_PER_STEP == 0 else 1
    rows = seqs * seq
    transposed_spec = pl.BlockSpec((LANES, rows), lambda hp, bi: (hp, bi))
    token_major_spec = pl.BlockSpec((rows, LANES), lambda hp, bi: (bi, hp))
    return pl.pallas_call(
        functools.partial(kernel, nb=nb, seqs=seqs),
        grid=(d // LANES, t // rows),
        in_specs=[transposed_spec, token_major_spec, transposed_spec] + extra_specs,
        out_specs=token_major_spec,
        out_shape=jax.ShapeDtypeStruct((t, d), _BF16),
        scratch_shapes=[pltpu.VMEM((HEADS_PER_STEP, seqs * nb, LANES, w), _BF16)]
                       + [pltpu.VMEM(shape(seqs), dt) for shape, dt in scratch]
                       + [pltpu.VMEM((seqs * nb, LANES, w), _F32)],
        compiler_params=pltpu.CompilerParams(
            dimension_semantics=("arbitrary", "arbitrary"),
            vmem_limit_bytes=VMEM_LIMIT_BYTES),
        name=name,
    )(q_t, k, v_t, *extra_inputs)


def _moba_attention(q_t, k, v_t, seq, bias_tiles, cfar):
    w = MOBA_BLOCK
    nb = seq // w
    topk = min(MOBA_TOPK, nb)
    extra_specs = [
        pl.BlockSpec((HEADS_PER_STEP, 2, w, w), lambda hp, bi: (hp, 0, 0, 0)),
        pl.BlockSpec((HEADS_PER_STEP, 8, w), lambda hp, bi: (hp, 0, 0)),
    ]
    return _attention_call(functools.partial(_moba_kernel, topk=topk), q_t, k, v_t, seq,
                           [bias_tiles, cfar], extra_specs, [], "moba_attention")


def _sb_kernel(q_ref, k_ref, v_ref, o_ref, qT_ref, r_ref, oT_ref, *, nb, seqs):
    w = MOBA_BLOCK
    _store_head_queries(q_ref, qT_ref, seqs * nb)

    key = lax.broadcasted_iota(jnp.int32, (w, w), 0)
    col = lax.broadcasted_iota(jnp.int32, (w, w), 1)
    causal = key < col
    upper = jnp.where(col > key, 1.0, 0.0).astype(_BF16)

    def run_tiles(tiles, resume):
        logits, partial, state, r_final = {}, {}, {}, {}
        first_key = {}
        for h, i, j in tiles:
            first_key.setdefault((h, i), j)
        last_key = {(h, i): j for h, i, j in tiles}

        def issue_logits(t):
            h, i, j = tiles[t]
            logits[t] = jnp.dot(k_ref[j * w:(j + 1) * w, :], qT_ref[h, i],
                                preferred_element_type=_F32)

        def issue_cumsum(t):
            h, i, j = tiles[t]
            z = logits.pop(t)
            sp = jnp.maximum(z, jnp.log(1.0 + jnp.exp2(jnp.minimum(z, EXP2_SAFE_MAX))) * LOG2E)
            if j == i:
                sp = jnp.where(causal, sp, 0.0)
            after = jnp.dot(upper, sp.astype(_BF16), preferred_element_type=_F32)
            partial[t] = (z - sp, after, sp[0:1, :])

        def weigh_values(t):
            h, i, j = tiles[t]
            log_sig, after, sp_first = partial.pop(t)
            a = jnp.exp2(log_sig - after)
            if j == i:
                a = jnp.where(causal, a, 0.0)
            pv = jnp.dot(v_ref[_head_slice(h), j * w:(j + 1) * w], a.astype(_BF16),
                         preferred_element_type=_F32)
            tile_total = after[0:1, :] + sp_first
            if j == i:
                acc, r = pv, tile_total
            else:
                if j == first_key[h, i]:
                    acc_old, r_old = oT_ref[i, _head_slice(h), :], r_ref[h, i]
                else:
                    acc_old, r_old = state[h]
                acc = acc_old + pv * jnp.exp2(-r_old)
                r = r_old + tile_total
            state[h] = (acc, r)
            if j == last_key[h, i]:
                _store_query_block(o_ref, oT_ref, h, i, acc)
                r_final[h, i] = r
                if not resume:
                    r_ref[h, i] = r

        _software_pipeline(len(tiles), [issue_logits, issue_cumsum, weigh_values], SB_LOOKAHEAD)
        return r_final

    r_near = run_tiles(_causal_tiles(nb, seqs, newest=SB_NEAR_TILES), resume=False)
    far_tiles = _causal_tiles(nb, seqs, newest=-SB_NEAR_TILES)
    if far_tiles:
        r_min = None
        for h, i in sorted({(h, i) for h, i, _ in far_tiles}):
            r_min = r_near[h, i] if r_min is None else jnp.minimum(r_min, r_near[h, i])

        @pl.when(jnp.min(r_min) <= SB_SKIP_LOG2)
        def _():
            run_tiles(far_tiles, resume=True)


def _sb_attention(q_t, k, v_t, seq):
    nb = seq // MOBA_BLOCK
    r_scratch = (lambda seqs: (HEADS_PER_STEP, seqs * nb, 1, MOBA_BLOCK), _F32)
    return _attention_call(_sb_kernel, q_t, k, v_t, seq, [], [], [r_scratch],
                           "stick_breaking_attention")


def kernel(x, rel_bias, w_qkv, w_o, ln_mix_g, ln_mix_b, w_up, w_down, ln_ffn_g, ln_ffn_b):
    b, s, d = x.shape
    depth = w_qkv.shape[0]
    alpha = (2.0 * depth) ** 0.25
    assert d % (HEADS_PER_STEP * HEAD_DIM) == 0
    bias_tiles, cfar = _bias_tiles(rel_bias)
    h = x.reshape(b * s, d)
    for i in range(depth):
        q_t, k, v_t = _qkv_proj(h, w_qkv[i])
        if i % 2 == 0:
            o = _moba_attention(q_t, k, v_t, s, bias_tiles, cfar)
        else:
            o = _sb_attention(q_t, k, v_t, s)
        h = _oproj_ln(o, h, w_o[i].astype(_BF16), ln_mix_g[i], ln_mix_b[i], alpha)
        h = _mlp_ln(h, w_up[i].astype(_BF16), w_down[i].astype(_BF16),
                    ln_ffn_g[i], ln_ffn_b[i], alpha)
    return h.reshape(b, s, d)
```

```python
import functools
import math

import numpy as np
import jax
import jax.numpy as jnp
from jax import lax
from jax.experimental import pallas as pl
from jax.experimental.pallas import tpu as pltpu

HEAD_DIM = 64
HEADS_PER_STEP = 2
LANES = 128
MOBA_BLOCK = 256
MOBA_TOPK = 3
REL_BUCKETS = 32
REL_MAX_DIST = 128
LN_EPS = 1e-5
QKV_TOKEN_TILE = 1024
MLP_TOKEN_TILE = 1024
EPILOGUE_ROW_SUBTILE = 256
FF_CHUNK = 1024
VMEM_LIMIT_BYTES = 56 * 1024 * 1024
POST_ATTENTION_VMEM_LIMIT_BYTES = 62 * 1024 * 1024
SEQS_PER_STEP = 2
MOBA_LOOKAHEAD = 6
SB_LOOKAHEAD = 2
SB_NEAR_TILES = 2
SB_SKIP_LOG2 = 152.0
LOG2E = math.log2(math.e)
EXP2_SAFE_MAX = 126.0
BF16_ROW_TILE = 16

_F32 = jnp.float32
_BF16 = jnp.bfloat16
_NEG_INF = float("-inf")


def _t5_bucket_upper_bounds():
    n = np.arange(0, 4 * REL_MAX_DIST)
    max_exact = REL_BUCKETS // 2
    nf = np.maximum(n, 1).astype(np.float64)
    large = max_exact + (np.log(nf / max_exact) / math.log(REL_MAX_DIST / max_exact)
                         * (REL_BUCKETS - max_exact)).astype(np.int64)
    large = np.minimum(large, REL_BUCKETS - 1)
    bucket = np.where(n < max_exact, n, large)
    assert bucket[-1] == REL_BUCKETS - 1 and np.all(np.diff(bucket) >= 0)
    return tuple(int(n[bucket == b].max()) for b in range(REL_BUCKETS - 1))


_BUCKET_UB = _t5_bucket_upper_bounds()
assert _BUCKET_UB[-1] < MOBA_BLOCK


def _layer_norm(y, g, b):
    mu = jnp.mean(y, axis=-1, keepdims=True)
    yc = y - mu
    var = jnp.mean(yc * yc, axis=-1, keepdims=True)
    return yc * lax.rsqrt(var + LN_EPS) * g + b


_CONTRACT_LAST = (((1,), (1,)), ((), ()))


def _qkv_kernel(x_ref, wqT_ref, wk_ref, wvT_ref, qT_ref, k_ref, vT_ref, *, q_scale):
    xb = x_ref[...].astype(_BF16)
    q_t = lax.dot_general(wqT_ref[...], xb, _CONTRACT_LAST, preferred_element_type=_F32)
    qT_ref[...] = (q_t * q_scale).astype(_BF16)
    k_ref[...] = jnp.dot(xb, wk_ref[...], preferred_element_type=_F32).astype(_BF16)
    v_t = lax.dot_general(wvT_ref[...], xb, _CONTRACT_LAST, preferred_element_type=_F32)
    vT_ref[...] = v_t.astype(_BF16)


def _qkv_proj(h2d, w_qkv):
    t, d = h2d.shape
    tm = min(QKV_TOKEN_TILE, t)
    assert t % tm == 0 and w_qkv.shape == (d, 3 * d)
    wq_t = w_qkv[:, :d].T.astype(_BF16)
    wk = w_qkv[:, d:2 * d].astype(_BF16)
    wv_t = w_qkv[:, 2 * d:].T.astype(_BF16)
    weight_spec = pl.BlockSpec((d, d), lambda i: (0, 0), pipeline_mode=pl.Buffered(1))
    transposed = jax.ShapeDtypeStruct((d, t), _BF16)
    return pl.pallas_call(
        functools.partial(_qkv_kernel, q_scale=LOG2E * HEAD_DIM ** -0.5),
        grid=(t // tm,),
        in_specs=[pl.BlockSpec((tm, d), lambda i: (i, 0)), weight_spec, weight_spec, weight_spec],
        out_specs=[pl.BlockSpec((d, tm), lambda i: (0, i)),
                   pl.BlockSpec((tm, d), lambda i: (i, 0)),
                   pl.BlockSpec((d, tm), lambda i: (0, i))],
        out_shape=[transposed, jax.ShapeDtypeStruct((t, d), _BF16), transposed],
        compiler_params=pltpu.CompilerParams(dimension_semantics=("arbitrary",),
                                             vmem_limit_bytes=VMEM_LIMIT_BYTES),
        name="qkv_proj",
    )(h2d, wq_t, wk, wv_t)


def _post_attention_kernel(o_ref, h_ref, wo_ref, g1_ref, b1_ref, wu_ref, wd_ref, g2_ref, b2_ref,
                           out_ref, *, alpha, d_ff, fc, sub):
    mixed, expanded = {}, {}

    def project(s):
        mixed[s] = jnp.dot(o_ref[s * sub:(s + 1) * sub, :], wo_ref[...],
                           preferred_element_type=_F32)

    def mix_channels(s):
        h1 = _layer_norm(alpha * h_ref[s * sub:(s + 1) * sub, :] + mixed.pop(s),
                         g1_ref[...], b1_ref[...])
        hb = h1.astype(_BF16)
        acc = alpha * h1
        for c in range(d_ff // fc):
            u = jnp.dot(hb, wu_ref[:, c * fc:(c + 1) * fc], preferred_element_type=_F32)
            u = jnp.square(jnp.maximum(u, 0.0)).astype(_BF16)
            acc = acc + jnp.dot(u, wd_ref[c * fc:(c + 1) * fc, :], preferred_element_type=_F32)
        expanded[s] = acc

    def normalize(s):
        out_ref[s * sub:(s + 1) * sub, :] = _layer_norm(expanded.pop(s), g2_ref[...], b2_ref[...])

    _software_pipeline(o_ref.shape[0] // sub, [project, mix_channels, normalize], 1)


def _post_attention(o2d, h2d, w_o, g1, b1, w_up, w_down, g2, b2, alpha):
    t, d = h2d.shape
    d_ff = w_up.shape[1]
    tm = min(MLP_TOKEN_TILE, t)
    sub = min(EPILOGUE_ROW_SUBTILE, tm)
    fc = min(FF_CHUNK, d_ff)
    assert t % tm == 0 and tm % sub == 0 and d_ff % fc == 0
    rows = pl.BlockSpec((tm, d), lambda i: (i, 0))
    vec = pl.BlockSpec((1, d), lambda i: (0, 0))

    def resident(shape):
        return pl.BlockSpec(shape, lambda i: (0, 0), pipeline_mode=pl.Buffered(1))

    return pl.pallas_call(
        functools.partial(_post_attention_kernel, alpha=alpha, d_ff=d_ff, fc=fc, sub=sub),
        grid=(t // tm,),
        in_specs=[rows, rows, resident((d, d)), vec, vec,
                  resident((d, d_ff)), resident((d_ff, d)), vec, vec],
        out_specs=rows,
        out_shape=jax.ShapeDtypeStruct((t, d), _F32),
        compiler_params=pltpu.CompilerParams(dimension_semantics=("arbitrary",),
                                             vmem_limit_bytes=POST_ATTENTION_VMEM_LIMIT_BYTES),
        name="oproj_mlp_ln",
    )(o2d, h2d, w_o.astype(_BF16), g1.reshape(1, d), b1.reshape(1, d),
      w_up.astype(_BF16), w_down.astype(_BF16), g2.reshape(1, d), b2.reshape(1, d))


def _bias_kernel(tab_ref, bias_ref, cfar_ref):
    h = pl.program_id(0)
    w = MOBA_BLOCK
    key = lax.broadcasted_iota(jnp.int32, (w, w), 0)
    qry = lax.broadcasted_iota(jnp.int32, (w, w), 1)
    far = tab_ref[REL_BUCKETS - 1, h] * LOG2E
    for d in range(2):
        dist = qry - key + d * w
        val = jnp.full((w, w), far, _F32)
        for bkt in range(REL_BUCKETS - 2, -1, -1):
            val = jnp.where(dist <= _BUCKET_UB[bkt], tab_ref[bkt, h] * LOG2E, val)
        if d == 0:
            val = jnp.where(dist >= 0, val, _NEG_INF)
        bias_ref[0, d] = val
    cfar_ref[0] = jnp.full((8, w), far, _F32)


def _bias_tiles(rel_bias):
    n_heads = rel_bias.shape[1]
    w = MOBA_BLOCK
    return pl.pallas_call(
        _bias_kernel,
        grid=(n_heads,),
        in_specs=[pl.BlockSpec(memory_space=pltpu.SMEM)],
        out_specs=[pl.BlockSpec((1, 2, w, w), lambda h: (h, 0, 0, 0)),
                   pl.BlockSpec((1, 8, w), lambda h: (h, 0, 0))],
        out_shape=[jax.ShapeDtypeStruct((n_heads, 2, w, w), _F32),
                   jax.ShapeDtypeStruct((n_heads, 8, w), _F32)],
        compiler_params=pltpu.CompilerParams(dimension_semantics=("arbitrary",)),
        name="t5_bias_tiles",
    )(rel_bias.astype(_F32))


def _head_rows(h):
    row = lax.broadcasted_iota(jnp.int32, (LANES, MOBA_BLOCK), 0)
    return (row // HEAD_DIM) == h


def _store_head_queries(q_ref, qT_ref, nb):
    w = MOBA_BLOCK
    for h in range(HEADS_PER_STEP):
        mine = _head_rows(h)
        for i in range(nb):
            q_t = q_ref[:, i * w:(i + 1) * w]
            qT_ref[h, i] = jnp.where(mine, q_t, jnp.zeros_like(q_t))


def _head_slice(h):
    return slice(h * HEAD_DIM, (h + 1) * HEAD_DIM)


def _store_query_block(o_ref, oT_ref, h, i, result):
    w = MOBA_BLOCK
    oT_ref[i, _head_slice(h), :] = result
    if h == HEADS_PER_STEP - 1:
        o_ref[i * w:(i + 1) * w, :] = oT_ref[i].T.astype(_BF16)


def _causal_tiles(nb, seqs, newest=None):
    tiles = []
    for base in range(0, seqs * nb, nb):
        for i in range(nb):
            keys = list(range(i, -1, -1))
            if newest is not None:
                keys = keys[:newest] if newest > 0 else keys[-newest:]
            tiles += [(h, base + i, base + j) for j in keys for h in range(HEADS_PER_STEP)]
    return tiles


def _software_pipeline(n_items, stages, lookahead):
    n_stages = len(stages)
    for step in range(n_items + (n_stages - 1) * lookahead):
        for s, stage in enumerate(stages):
            t = step - s * lookahead
            if 0 <= t < n_items:
                stage(t)


def _moba_kernel(q_ref, k_ref, v_ref, bias_ref, cfar_ref, o_ref,
                 qT_ref, oT_ref, *, nb, seqs, topk):
    w = MOBA_BLOCK
    _store_head_queries(q_ref, qT_ref, seqs * nb)

    ones_rows = jnp.where(lax.broadcasted_iota(jnp.int32, (BF16_ROW_TILE, w), 0) == 0,
                          1.0, 0.0).astype(_BF16)

    def values_and_ones(h, j):
        return jnp.concatenate([v_ref[_head_slice(h), j * w:(j + 1) * w], ones_rows], axis=0)

    jidx = lax.broadcasted_iota(jnp.int32, (8, w), 0)
    addrows = {}
    for base in range(0, seqs * nb, nb):
        rows = [jnp.mean(k_ref[(base + j) * w:(base + j + 1) * w, :].astype(_F32),
                         axis=0, keepdims=True) for j in range(nb)]
        kmean = jnp.concatenate(rows + [jnp.zeros((16 - nb, LANES), _F32)], axis=0)
        p0 = kmean.astype(_BF16)
        r1 = kmean - p0.astype(_F32)
        p1 = r1.astype(_BF16)
        p2 = (r1 - p1.astype(_F32)).astype(_BF16)
        kparts = jnp.concatenate([p0, p1, p2], axis=0)
        for h in range(HEADS_PER_STEP):
            cfar = cfar_ref[h][0:1, :]
            for i in range(1, nb):
                g3 = jnp.dot(kparts, qT_ref[h, base + i], preferred_element_type=_F32)
                gate = g3[0:8] + g3[16:24] + g3[32:40]
                cnt = jnp.zeros((8, w), jnp.int32)
                for jp in range(i):
                    rowv = gate[jp:jp + 1, :]
                    beats = (rowv > gate) | ((rowv == gate) & (jp < jidx))
                    cnt = cnt + jnp.where(beats, 1, 0)
                sel = (cnt < topk) & (jidx < i)
                addrows[h, base + i] = jnp.where(sel, jnp.where(jidx < i - 1, cfar, 0.0),
                                                 _NEG_INF)

    tiles = _causal_tiles(nb, seqs)
    scores = {}
    state = {}

    def issue_scores(t):
        h, i, j = tiles[t]
        s = jnp.dot(k_ref[j * w:(j + 1) * w, :], qT_ref[h, i], preferred_element_type=_F32)
        if j >= i - 1:
            s = s + bias_ref[h, i - j]
        scores[t] = s

    def softmax_accumulate(t):
        h, i, j = tiles[t]
        s = scores.pop(t)
        mt = jnp.max(s, axis=0, keepdims=True)
        if j == i:
            m = mt
            p = jnp.exp2(s - m)
            acc = jnp.dot(values_and_ones(h, j), p.astype(_BF16), preferred_element_type=_F32)
        else:
            arow = addrows[h, i][j % nb:j % nb + 1, :]
            m_old, acc_old = state[h]
            m = jnp.maximum(m_old, mt + arow)
            a = jnp.exp2(m_old - m)
            p = jnp.exp2(s - (m - arow))
            acc = a * acc_old + jnp.dot(values_and_ones(h, j), p.astype(_BF16),
                                        preferred_element_type=_F32)
        state[h] = (m, acc)
        if j % nb == 0:
            denom = acc[HEAD_DIM:HEAD_DIM + 1, :]
            _store_query_block(o_ref, oT_ref, h, i, acc[0:HEAD_DIM, :] * (1.0 / denom))

    _software_pipeline(len(tiles), [issue_scores, softmax_accumulate], MOBA_LOOKAHEAD)


def _attention_call(kernel, q_t, k, v_t, seq, extra_inputs, extra_specs, scratch, name):
    d, t = q_t.shape
    w = MOBA_BLOCK
    assert seq % w == 0 and t % seq == 0 and d % LANES == 0
    nb = seq // w
    assert nb <= 8
    seqs = SEQS_PER_STEP if (t // seq) % SEQS_PER_STEP == 0 else 1
    rows = seqs * seq
    transposed_spec = pl.BlockSpec((LANES, rows), lambda hp, bi: (hp, bi))
    token_major_spec = pl.BlockSpec((rows, LANES), lambda hp, bi: (bi, hp))
    return pl.pallas_call(
        functools.partial(kernel, nb=nb, seqs=seqs),
        grid=(d // LANES, t // rows),
        in_specs=[transposed_spec, token_major_spec, transposed_spec] + extra_specs,
        out_specs=token_major_spec,
        out_shape=jax.ShapeDtypeStruct((t, d), _BF16),
        scratch_shapes=[pltpu.VMEM((HEADS_PER_STEP, seqs * nb, LANES, w), _BF16)]
                       + [pltpu.VMEM(shape(seqs), dt) for shape, dt in scratch]
                       + [pltpu.VMEM((seqs * nb, LANES, w), _F32)],
        compiler_params=pltpu.CompilerParams(
            dimension_semantics=("arbitrary", "arbitrary"),
            vmem_limit_bytes=VMEM_LIMIT_BYTES),
        name=name,
    )(q_t, k, v_t, *extra_inputs)


def _moba_attention(q_t, k, v_t, seq, bias_tiles, cfar):
    w = MOBA_BLOCK
    nb = seq // w
    topk = min(MOBA_TOPK, nb)
    extra_specs = [
        pl.BlockSpec((HEADS_PER_STEP, 2, w, w), lambda hp, bi: (hp, 0, 0, 0)),
        pl.BlockSpec((HEADS_PER_STEP, 8, w), lambda hp, bi: (hp, 0, 0)),
    ]
    return _attention_call(functools.partial(_moba_kernel, topk=topk), q_t, k, v_t, seq,
                           [bias_tiles, cfar], extra_specs, [], "moba_attention")


def _sb_kernel(q_ref, k_ref, v_ref, o_ref, qT_ref, r_ref, oT_ref, *, nb, seqs):
    w = MOBA_BLOCK
    _store_head_queries(q_ref, qT_ref, seqs * nb)

    key = lax.broadcasted_iota(jnp.int32, (w, w), 0)
    col = lax.broadcasted_iota(jnp.int32, (w, w), 1)
    causal = key < col
    upper = jnp.where(col > key, 1.0, 0.0).astype(_BF16)

    def run_tiles(tiles, resume):
        logits, partial, state, r_final = {}, {}, {}, {}
        first_key = {}
        for h, i, j in tiles:
            first_key.setdefault((h, i), j)
        last_key = {(h, i): j for h, i, j in tiles}

        def issue_logits(t):
            h, i, j = tiles[t]
            logits[t] = jnp.dot(k_ref[j * w:(j + 1) * w, :], qT_ref[h, i],
                                preferred_element_type=_F32)

        def issue_cumsum(t):
            h, i, j = tiles[t]
            z = logits.pop(t)
            sp = jnp.maximum(z, jnp.log(1.0 + jnp.exp2(jnp.minimum(z, EXP2_SAFE_MAX))) * LOG2E)
            if j == i:
                sp = jnp.where(causal, sp, 0.0)
            after = jnp.dot(upper, sp.astype(_BF16), preferred_element_type=_F32)
            partial[t] = (z - sp, after, sp[0:1, :])

        def weigh_values(t):
            h, i, j = tiles[t]
            log_sig, after, sp_first = partial.pop(t)
            a = jnp.exp2(log_sig - after)
            if j == i:
                a = jnp.where(causal, a, 0.0)
            pv = jnp.dot(v_ref[_head_slice(h), j * w:(j + 1) * w], a.astype(_BF16),
                         preferred_element_type=_F32)
            tile_total = after[0:1, :] + sp_first
            if j == i:
                acc, r = pv, tile_total
            else:
                if j == first_key[h, i]:
                    acc_old, r_old = oT_ref[i, _head_slice(h), :], r_ref[h, i]
                else:
                    acc_old, r_old = state[h]
                acc = acc_old + pv * jnp.exp2(-r_old)
                r = r_old + tile_total
            state[h] = (acc, r)
            if j == last_key[h, i]:
                _store_query_block(o_ref, oT_ref, h, i, acc)
                r_final[h, i] = r
                if not resume:
                    r_ref[h, i] = r

        _software_pipeline(len(tiles), [issue_logits, issue_cumsum, weigh_values], SB_LOOKAHEAD)
        return r_final

    r_near = run_tiles(_causal_tiles(nb, seqs, newest=SB_NEAR_TILES), resume=False)
    far_tiles = _causal_tiles(nb, seqs, newest=-SB_NEAR_TILES)
    if far_tiles:
        r_min = None
        for h, i in sorted({(h, i) for h, i, _ in far_tiles}):
            r_min = r_near[h, i] if r_min is None else jnp.minimum(r_min, r_near[h, i])

        @pl.when(jnp.min(r_min) <= SB_SKIP_LOG2)
        def _():
            run_tiles(far_tiles, resume=True)


def _sb_attention(q_t, k, v_t, seq):
    nb = seq // MOBA_BLOCK
    r_scratch = (lambda seqs: (HEADS_PER_STEP, seqs * nb, 1, MOBA_BLOCK), _F32)
    return _attention_call(_sb_kernel, q_t, k, v_t, seq, [], [], [r_scratch],
                           "stick_breaking_attention")


def kernel(x, rel_bias, w_qkv, w_o, ln_mix_g, ln_mix_b, w_up, w_down, ln_ffn_g, ln_ffn_b):
    b, s, d = x.shape
    depth = w_qkv.shape[0]
    alpha = (2.0 * depth) ** 0.25
    assert d % (HEADS_PER_STEP * HEAD_DIM) == 0
    bias_tiles, cfar = _bias_tiles(rel_bias)
    h = x.reshape(b * s, d)
    for i in range(depth):
        q_t, k, v_t = _qkv_proj(h, w_qkv[i])
        if i % 2 == 0:
            o = _moba_attention(q_t, k, v_t, s, bias_tiles, cfar)
        else:
            o = _sb_attention(q_t, k, v_t, s)
        h = _post_attention(o, h, w_o[i], ln_mix_g[i], ln_mix_b[i],
                            w_up[i], w_down[i], ln_ffn_g[i], ln_ffn_b[i], alpha)
    return h.reshape(b, s, d)
```

```python
import functools
import math

import numpy as np
import jax
import jax.numpy as jnp
from jax import lax
from jax.experimental import pallas as pl
from jax.experimental.pallas import tpu as pltpu

HEAD_DIM = 64
HEADS_PER_STEP = 2
LANES = 128
MOBA_BLOCK = 256
MOBA_TOPK = 3
REL_BUCKETS = 32
REL_MAX_DIST = 128
LN_EPS = 1e-5
QKV_TOKEN_TILE = 1024
MLP_TOKEN_TILE = 1024
EPILOGUE_ROW_SUBTILE = 256
FF_CHUNK = 1024
VMEM_LIMIT_BYTES = 56 * 1024 * 1024
POST_ATTENTION_VMEM_LIMIT_BYTES = 62 * 1024 * 1024
SEQS_PER_STEP = 2
MOBA_LOOKAHEAD = 6
SB_LOOKAHEAD = 2
SB_NEAR_TILES = 2
SB_SKIP_LOG2 = 152.0
LOG2E = math.log2(math.e)
EXP2_SAFE_MAX = 126.0
BF16_ROW_TILE = 16

_F32 = jnp.float32
_BF16 = jnp.bfloat16
_NEG_INF = float("-inf")


def _t5_bucket_upper_bounds():
    n = np.arange(0, 4 * REL_MAX_DIST)
    max_exact = REL_BUCKETS // 2
    nf = np.maximum(n, 1).astype(np.float64)
    large = max_exact + (np.log(nf / max_exact) / math.log(REL_MAX_DIST / max_exact)
                         * (REL_BUCKETS - max_exact)).astype(np.int64)
    large = np.minimum(large, REL_BUCKETS - 1)
    bucket = np.where(n < max_exact, n, large)
    assert bucket[-1] == REL_BUCKETS - 1 and np.all(np.diff(bucket) >= 0)
    return tuple(int(n[bucket == b].max()) for b in range(REL_BUCKETS - 1))


_BUCKET_UB = _t5_bucket_upper_bounds()
assert _BUCKET_UB[-1] < MOBA_BLOCK


def _layer_norm(y, g, b):
    mu = jnp.mean(y, axis=-1, keepdims=True)
    yc = y - mu
    var = jnp.mean(yc * yc, axis=-1, keepdims=True)
    return yc * lax.rsqrt(var + LN_EPS) * g + b


_CONTRACT_LAST = (((1,), (1,)), ((), ()))


def _qkv_kernel(x_ref, wqT_ref, wk_ref, wvT_ref, qT_ref, k_ref, vT_ref, *, q_scale):
    xb = x_ref[...].astype(_BF16)
    q_t = lax.dot_general(wqT_ref[...], xb, _CONTRACT_LAST, preferred_element_type=_F32)
    qT_ref[...] = (q_t * q_scale).astype(_BF16)
    k_ref[...] = jnp.dot(xb, wk_ref[...], preferred_element_type=_F32).astype(_BF16)
    v_t = lax.dot_general(wvT_ref[...], xb, _CONTRACT_LAST, preferred_element_type=_F32)
    vT_ref[...] = v_t.astype(_BF16)


def _split_qkv_weights(w_qkv):
    d = w_qkv.shape[1]
    wq_t = jnp.swapaxes(w_qkv[:, :, :d], 1, 2).astype(_BF16)
    wk = w_qkv[:, :, d:2 * d].astype(_BF16)
    wv_t = jnp.swapaxes(w_qkv[:, :, 2 * d:], 1, 2).astype(_BF16)
    return wq_t, wk, wv_t


def _layer_weight_spec(layer, shape):
    return pl.BlockSpec((None,) + shape, lambda i: (layer, 0, 0), pipeline_mode=pl.Buffered(1))


def _qkv_proj(h2d, qkv_weights, layer):
    t, d = h2d.shape
    tm = min(QKV_TOKEN_TILE, t)
    wq_t, wk, wv_t = qkv_weights
    assert t % tm == 0 and wk.shape[1:] == (d, d)
    weight_spec = _layer_weight_spec(layer, (d, d))
    transposed = jax.ShapeDtypeStruct((d, t), _BF16)
    return pl.pallas_call(
        functools.partial(_qkv_kernel, q_scale=LOG2E * HEAD_DIM ** -0.5),
        grid=(t // tm,),
        in_specs=[pl.BlockSpec((tm, d), lambda i: (i, 0)), weight_spec, weight_spec, weight_spec],
        out_specs=[pl.BlockSpec((d, tm), lambda i: (0, i)),
                   pl.BlockSpec((tm, d), lambda i: (i, 0)),
                   pl.BlockSpec((d, tm), lambda i: (0, i))],
        out_shape=[transposed, jax.ShapeDtypeStruct((t, d), _BF16), transposed],
        compiler_params=pltpu.CompilerParams(dimension_semantics=("arbitrary",),
                                             vmem_limit_bytes=VMEM_LIMIT_BYTES),
        name="qkv_proj",
    )(h2d, wq_t, wk, wv_t)


def _post_attention_kernel(o_ref, h_ref, wo_ref, g1_ref, b1_ref, wu_ref, wd_ref, g2_ref, b2_ref,
                           out_ref, *, alpha, d_ff, fc, sub):
    mixed, expanded = {}, {}

    def project(s):
        mixed[s] = jnp.dot(o_ref[s * sub:(s + 1) * sub, :], wo_ref[...],
                           preferred_element_type=_F32)

    def mix_channels(s):
        h1 = _layer_norm(alpha * h_ref[s * sub:(s + 1) * sub, :] + mixed.pop(s),
                         g1_ref[...], b1_ref[...])
        hb = h1.astype(_BF16)
        acc = alpha * h1
        for c in range(d_ff // fc):
            u = jnp.dot(hb, wu_ref[:, c * fc:(c + 1) * fc], preferred_element_type=_F32)
            u = jnp.square(jnp.maximum(u, 0.0)).astype(_BF16)
            acc = acc + jnp.dot(u, wd_ref[c * fc:(c + 1) * fc, :], preferred_element_type=_F32)
        expanded[s] = acc

    def normalize(s):
        out_ref[s * sub:(s + 1) * sub, :] = _layer_norm(expanded.pop(s), g2_ref[...], b2_ref[...])

    _software_pipeline(o_ref.shape[0] // sub, [project, mix_channels, normalize], 1)


def _post_attention(o2d, h2d, layer, w_o, g1, b1, w_up, w_down, g2, b2, alpha):
    t, d = h2d.shape
    d_ff = w_up.shape[2]
    tm = min(MLP_TOKEN_TILE, t)
    sub = min(EPILOGUE_ROW_SUBTILE, tm)
    fc = min(FF_CHUNK, d_ff)
    assert t % tm == 0 and tm % sub == 0 and d_ff % fc == 0
    rows = pl.BlockSpec((tm, d), lambda i: (i, 0))
    vec = pl.BlockSpec((1, d), lambda i: (0, 0))

    def resident(shape):
        return _layer_weight_spec(layer, shape)

    return pl.pallas_call(
        functools.partial(_post_attention_kernel, alpha=alpha, d_ff=d_ff, fc=fc, sub=sub),
        grid=(t // tm,),
        in_specs=[rows, rows, resident((d, d)), vec, vec,
                  resident((d, d_ff)), resident((d_ff, d)), vec, vec],
        out_specs=rows,
        out_shape=jax.ShapeDtypeStruct((t, d), _F32),
        compiler_params=pltpu.CompilerParams(dimension_semantics=("arbitrary",),
                                             vmem_limit_bytes=POST_ATTENTION_VMEM_LIMIT_BYTES),
        name="oproj_mlp_ln",
    )(o2d, h2d, w_o, g1.reshape(1, d), b1.reshape(1, d), w_up, w_down,
      g2.reshape(1, d), b2.reshape(1, d))


def _bias_kernel(tab_ref, bias_ref, cfar_ref):
    h = pl.program_id(0)
    w = MOBA_BLOCK
    key = lax.broadcasted_iota(jnp.int32, (w, w), 0)
    qry = lax.broadcasted_iota(jnp.int32, (w, w), 1)
    far = tab_ref[REL_BUCKETS - 1, h] * LOG2E
    for d in range(2):
        dist = qry - key + d * w
        val = jnp.full((w, w), far, _F32)
        for bkt in range(REL_BUCKETS - 2, -1, -1):
            val = jnp.where(dist <= _BUCKET_UB[bkt], tab_ref[bkt, h] * LOG2E, val)
        if d == 0:
            val = jnp.where(dist >= 0, val, _NEG_INF)
        bias_ref[0, d] = val
    cfar_ref[0] = jnp.full((8, w), far, _F32)


def _bias_tiles(rel_bias):
    n_heads = rel_bias.shape[1]
    w = MOBA_BLOCK
    return pl.pallas_call(
        _bias_kernel,
        grid=(n_heads,),
        in_specs=[pl.BlockSpec(memory_space=pltpu.SMEM)],
        out_specs=[pl.BlockSpec((1, 2, w, w), lambda h: (h, 0, 0, 0)),
                   pl.BlockSpec((1, 8, w), lambda h: (h, 0, 0))],
        out_shape=[jax.ShapeDtypeStruct((n_heads, 2, w, w), _F32),
                   jax.ShapeDtypeStruct((n_heads, 8, w), _F32)],
        compiler_params=pltpu.CompilerParams(dimension_semantics=("arbitrary",)),
        name="t5_bias_tiles",
    )(rel_bias.astype(_F32))


def _head_rows(h):
    row = lax.broadcasted_iota(jnp.int32, (LANES, MOBA_BLOCK), 0)
    return (row // HEAD_DIM) == h


def _store_head_queries(q_ref, qT_ref, nb):
    w = MOBA_BLOCK
    for h in range(HEADS_PER_STEP):
        mine = _head_rows(h)
        for i in range(nb):
            q_t = q_ref[:, i * w:(i + 1) * w]
            qT_ref[h, i] = jnp.where(mine, q_t, jnp.zeros_like(q_t))


def _head_slice(h):
    return slice(h * HEAD_DIM, (h + 1) * HEAD_DIM)


def _store_query_block(o_ref, oT_ref, h, i, result):
    w = MOBA_BLOCK
    oT_ref[i, _head_slice(h), :] = result
    if h == HEADS_PER_STEP - 1:
        o_ref[i * w:(i + 1) * w, :] = oT_ref[i].T.astype(_BF16)


def _causal_tiles(nb, seqs, newest=None):
    tiles = []
    for base in range(0, seqs * nb, nb):
        for i in range(nb):
            keys = list(range(i, -1, -1))
            if newest is not None:
                keys = keys[:newest] if newest > 0 else keys[-newest:]
            tiles += [(h, base + i, base + j) for j in keys for h in range(HEADS_PER_STEP)]
    return tiles


def _software_pipeline(n_items, stages, lookahead):
    n_stages = len(stages)
    for step in range(n_items + (n_stages - 1) * lookahead):
        for s, stage in enumerate(stages):
            t = step - s * lookahead
            if 0 <= t < n_items:
                stage(t)


def _moba_kernel(q_ref, k_ref, v_ref, bias_ref, cfar_ref, o_ref,
                 qT_ref, oT_ref, *, nb, seqs, topk):
    w = MOBA_BLOCK
    _store_head_queries(q_ref, qT_ref, seqs * nb)

    ones_rows = jnp.where(lax.broadcasted_iota(jnp.int32, (BF16_ROW_TILE, w), 0) == 0,
                          1.0, 0.0).astype(_BF16)

    def values_and_ones(h, j):
        return jnp.concatenate([v_ref[_head_slice(h), j * w:(j + 1) * w], ones_rows], axis=0)

    jidx = lax.broadcasted_iota(jnp.int32, (8, w), 0)
    addrows = {}
    for base in range(0, seqs * nb, nb):
        rows = [jnp.mean(k_ref[(base + j) * w:(base + j + 1) * w, :].astype(_F32),
                         axis=0, keepdims=True) for j in range(nb)]
        kmean = jnp.concatenate(rows + [jnp.zeros((16 - nb, LANES), _F32)], axis=0)
        p0 = kmean.astype(_BF16)
        r1 = kmean - p0.astype(_F32)
        p1 = r1.astype(_BF16)
        p2 = (r1 - p1.astype(_F32)).astype(_BF16)
        kparts = jnp.concatenate([p0, p1, p2], axis=0)
        for h in range(HEADS_PER_STEP):
            cfar = cfar_ref[h][0:1, :]
            for i in range(1, nb):
                g3 = jnp.dot(kparts, qT_ref[h, base + i], preferred_element_type=_F32)
                gate = g3[0:8] + g3[16:24] + g3[32:40]
                cnt = jnp.zeros((8, w), jnp.int32)
                for jp in range(i):
                    rowv = gate[jp:jp + 1, :]
                    beats = (rowv > gate) | ((rowv == gate) & (jp < jidx))
                    cnt = cnt + jnp.where(beats, 1, 0)
                sel = (cnt < topk) & (jidx < i)
                addrows[h, base + i] = jnp.where(sel, jnp.where(jidx < i - 1, cfar, 0.0),
                                                 _NEG_INF)

    tiles = _causal_tiles(nb, seqs)
    scores = {}
    state = {}

    def issue_scores(t):
        h, i, j = tiles[t]
        s = jnp.dot(k_ref[j * w:(j + 1) * w, :], qT_ref[h, i], preferred_element_type=_F32)
        if j >= i - 1:
            s = s + bias_ref[h, i - j]
        scores[t] = s

    def softmax_accumulate(t):
        h, i, j = tiles[t]
        s = scores.pop(t)
        mt = jnp.max(s, axis=0, keepdims=True)
        if j == i:
            m = mt
            p = jnp.exp2(s - m)
            acc = jnp.dot(values_and_ones(h, j), p.astype(_BF16), preferred_element_type=_F32)
        else:
            arow = addrows[h, i][j % nb:j % nb + 1, :]
            m_old, acc_old = state[h]
            m = jnp.maximum(m_old, mt + arow)
            a = jnp.exp2(m_old - m)
            p = jnp.exp2(s - (m - arow))
            acc = a * acc_old + jnp.dot(values_and_ones(h, j), p.astype(_BF16),
                                        preferred_element_type=_F32)
        state[h] = (m, acc)
        if j % nb == 0:
            denom = acc[HEAD_DIM:HEAD_DIM + 1, :]
            _store_query_block(o_ref, oT_ref, h, i, acc[0:HEAD_DIM, :] * (1.0 / denom))

    _software_pipeline(len(tiles), [issue_scores, softmax_accumulate], MOBA_LOOKAHEAD)


def _attention_call(kernel, q_t, k, v_t, seq, extra_inputs, extra_specs, scratch, name):
    d, t = q_t.shape
    w = MOBA_BLOCK
    assert seq % w == 0 and t % seq == 0 and d % LANES == 0
    nb = seq // w
    assert nb <= 8
    seqs = SEQS_PER_STEP if (t // seq) % SEQS_PER_STEP == 0 else 1
    rows = seqs * seq
    transposed_spec = pl.BlockSpec((LANES, rows), lambda hp, bi: (hp, bi))
    token_major_spec = pl.BlockSpec((rows, LANES), lambda hp, bi: (bi, hp))
    return pl.pallas_call(
        functools.partial(kernel, nb=nb, seqs=seqs),
        grid=(d // LANES, t // rows),
        in_specs=[transposed_spec, token_major_spec, transposed_spec] + extra_specs,
        out_specs=token_major_spec,
        out_shape=jax.ShapeDtypeStruct((t, d), _BF16),
        scratch_shapes=[pltpu.VMEM((HEADS_PER_STEP, seqs * nb, LANES, w), _BF16)]
                       + [pltpu.VMEM(shape(seqs), dt) for shape, dt in scratch]
                       + [pltpu.VMEM((seqs * nb, LANES, w), _F32)],
        compiler_params=pltpu.CompilerParams(
            dimension_semantics=("arbitrary", "arbitrary"),
            vmem_limit_bytes=VMEM_LIMIT_BYTES),
        name=name,
    )(q_t, k, v_t, *extra_inputs)


def _moba_attention(q_t, k, v_t, seq, bias_tiles, cfar):
    w = MOBA_BLOCK
    nb = seq // w
    topk = min(MOBA_TOPK, nb)
    extra_specs = [
        pl.BlockSpec((HEADS_PER_STEP, 2, w, w), lambda hp, bi: (hp, 0, 0, 0)),
        pl.BlockSpec((HEADS_PER_STEP, 8, w), lambda hp, bi: (hp, 0, 0)),
    ]
    return _attention_call(functools.partial(_moba_kernel, topk=topk), q_t, k, v_t, seq,
                           [bias_tiles, cfar], extra_specs, [], "moba_attention")


def _sb_kernel(q_ref, k_ref, v_ref, o_ref, qT_ref, r_ref, oT_ref, *, nb, seqs):
    w = MOBA_BLOCK
    _store_head_queries(q_ref, qT_ref, seqs * nb)

    key = lax.broadcasted_iota(jnp.int32, (w, w), 0)
    col = lax.broadcasted_iota(jnp.int32, (w, w), 1)
    causal = key < col
    upper = jnp.where(col > key, 1.0, 0.0).astype(_BF16)

    def run_tiles(tiles, resume):
        logits, partial, state, r_final = {}, {}, {}, {}
        first_key = {}
        for h, i, j in tiles:
            first_key.setdefault((h, i), j)
        last_key = {(h, i): j for h, i, j in tiles}

        def issue_logits(t):
            h, i, j = tiles[t]
            logits[t] = jnp.dot(k_ref[j * w:(j + 1) * w, :], qT_ref[h, i],
                                preferred_element_type=_F32)

        def issue_cumsum(t):
            h, i, j = tiles[t]
            z = logits.pop(t)
            sp = jnp.maximum(z, jnp.log(1.0 + jnp.exp2(jnp.minimum(z, EXP2_SAFE_MAX))) * LOG2E)
            if j == i:
                sp = jnp.where(causal, sp, 0.0)
            after = jnp.dot(upper, sp.astype(_BF16), preferred_element_type=_F32)
            partial[t] = (z - sp, after, sp[0:1, :])

        def weigh_values(t):
            h, i, j = tiles[t]
            log_sig, after, sp_first = partial.pop(t)
            a = jnp.exp2(log_sig - after)
            if j == i:
                a = jnp.where(causal, a, 0.0)
            pv = jnp.dot(v_ref[_head_slice(h), j * w:(j + 1) * w], a.astype(_BF16),
                         preferred_element_type=_F32)
            tile_total = after[0:1, :] + sp_first
            if j == i:
                acc, r = pv, tile_total
            else:
                if j == first_key[h, i]:
                    acc_old, r_old = oT_ref[i, _head_slice(h), :], r_ref[h, i]
                else:
                    acc_old, r_old = state[h]
                acc = acc_old + pv * jnp.exp2(-r_old)
                r = r_old + tile_total
            state[h] = (acc, r)
            if j == last_key[h, i]:
                _store_query_block(o_ref, oT_ref, h, i, acc)
                r_final[h, i] = r
                if not resume:
                    r_ref[h, i] = r

        _software_pipeline(len(tiles), [issue_logits, issue_cumsum, weigh_values], SB_LOOKAHEAD)
        return r_final

    r_near = run_tiles(_causal_tiles(nb, seqs, newest=SB_NEAR_TILES), resume=False)
    far_tiles = _causal_tiles(nb, seqs, newest=-SB_NEAR_TILES)
    if far_tiles:
        r_min = None
        for h, i in sorted({(h, i) for h, i, _ in far_tiles}):
            r_min = r_near[h, i] if r_min is None else jnp.minimum(r_min, r_near[h, i])

        @pl.when(jnp.min(r_min) <= SB_SKIP_LOG2)
        def _():
            run_tiles(far_tiles, resume=True)


def _sb_attention(q_t, k, v_t, seq):
    nb = seq // MOBA_BLOCK
    r_scratch = (lambda seqs: (HEADS_PER_STEP, seqs * nb, 1, MOBA_BLOCK), _F32)
    return _attention_call(_sb_kernel, q_t, k, v_t, seq, [], [], [r_scratch],
                           "stick_breaking_attention")


def kernel(x, rel_bias, w_qkv, w_o, ln_mix_g, ln_mix_b, w_up, w_down, ln_ffn_g, ln_ffn_b):
    b, s, d = x.shape
    depth = w_qkv.shape[0]
    alpha = (2.0 * depth) ** 0.25
    assert d % (HEADS_PER_STEP * HEAD_DIM) == 0
    bias_tiles, cfar = _bias_tiles(rel_bias)
    qkv_weights = _split_qkv_weights(w_qkv)
    w_o, w_up, w_down = w_o.astype(_BF16), w_up.astype(_BF16), w_down.astype(_BF16)
    h = x.reshape(b * s, d)
    for i in range(depth):
        q_t, k, v_t = _qkv_proj(h, qkv_weights, i)
        if i % 2 == 0:
            o = _moba_attention(q_t, k, v_t, s, bias_tiles, cfar)
        else:
            o = _sb_attention(q_t, k, v_t, s)
        h = _post_attention(o, h, i, w_o, ln_mix_g[i], ln_mix_b[i],
                            w_up, w_down, ln_ffn_g[i], ln_ffn_b[i], alpha)
    return h.reshape(b, s, d)
```

```python
import functools
import math

import numpy as np
import jax
import jax.numpy as jnp
from jax import lax
from jax.experimental import pallas as pl
from jax.experimental.pallas import tpu as pltpu

HEAD_DIM = 64
HEADS_PER_STEP = 2
LANES = 128
MOBA_BLOCK = 256
MOBA_TOPK = 3
REL_BUCKETS = 32
REL_MAX_DIST = 128
LN_EPS = 1e-5
QKV_TOKEN_TILE = 2048
MLP_TOKEN_TILE = 1024
EPILOGUE_ROW_SUBTILE = 256
FF_CHUNK = 1024
VMEM_LIMIT_BYTES = 56 * 1024 * 1024
POST_ATTENTION_VMEM_LIMIT_BYTES = 62 * 1024 * 1024
SEQS_PER_STEP = 2
MOBA_LOOKAHEAD = 6
SB_LOOKAHEAD = 2
SB_NEAR_TILES = 2
SB_SKIP_LOG2 = 152.0
LOG2E = math.log2(math.e)
EXP2_SAFE_MAX = 126.0
BF16_ROW_TILE = 16

_F32 = jnp.float32
_BF16 = jnp.bfloat16
_NEG_INF = float("-inf")


def _t5_bucket_upper_bounds():
    n = np.arange(0, 4 * REL_MAX_DIST)
    max_exact = REL_BUCKETS // 2
    nf = np.maximum(n, 1).astype(np.float64)
    large = max_exact + (np.log(nf / max_exact) / math.log(REL_MAX_DIST / max_exact)
                         * (REL_BUCKETS - max_exact)).astype(np.int64)
    large = np.minimum(large, REL_BUCKETS - 1)
    bucket = np.where(n < max_exact, n, large)
    assert bucket[-1] == REL_BUCKETS - 1 and np.all(np.diff(bucket) >= 0)
    return tuple(int(n[bucket == b].max()) for b in range(REL_BUCKETS - 1))


_BUCKET_UB = _t5_bucket_upper_bounds()
assert _BUCKET_UB[-1] < MOBA_BLOCK


def _layer_norm(y, g, b):
    mu = jnp.mean(y, axis=-1, keepdims=True)
    yc = y - mu
    var = jnp.mean(yc * yc, axis=-1, keepdims=True)
    return yc * lax.rsqrt(var + LN_EPS) * g + b


_CONTRACT_LAST = (((1,), (1,)), ((), ()))


def _qkv_kernel(x_ref, wqT_ref, wk_ref, wvT_ref, qT_ref, k_ref, vT_ref, *, q_scale):
    xb = x_ref[...].astype(_BF16)
    q_t = lax.dot_general(wqT_ref[...], xb, _CONTRACT_LAST, preferred_element_type=_F32)
    qT_ref[...] = (q_t * q_scale).astype(_BF16)
    k_ref[...] = jnp.dot(xb, wk_ref[...], preferred_element_type=_F32).astype(_BF16)
    v_t = lax.dot_general(wvT_ref[...], xb, _CONTRACT_LAST, preferred_element_type=_F32)
    vT_ref[...] = v_t.astype(_BF16)


def _split_qkv_weights(w_qkv):
    d = w_qkv.shape[1]
    wq_t = jnp.swapaxes(w_qkv[:, :, :d], 1, 2).astype(_BF16)
    wk = w_qkv[:, :, d:2 * d].astype(_BF16)
    wv_t = jnp.swapaxes(w_qkv[:, :, 2 * d:], 1, 2).astype(_BF16)
    return wq_t, wk, wv_t


def _layer_weight_spec(layer, shape):
    return pl.BlockSpec((None,) + shape, lambda i: (layer, 0, 0), pipeline_mode=pl.Buffered(1))


def _qkv_proj(h2d, qkv_weights, layer):
    t, d = h2d.shape
    tm = min(QKV_TOKEN_TILE, t)
    wq_t, wk, wv_t = qkv_weights
    assert t % tm == 0 and wk.shape[1:] == (d, d)
    weight_spec = _layer_weight_spec(layer, (d, d))
    transposed = jax.ShapeDtypeStruct((d, t), _BF16)
    return pl.pallas_call(
        functools.partial(_qkv_kernel, q_scale=LOG2E * HEAD_DIM ** -0.5),
        grid=(t // tm,),
        in_specs=[pl.BlockSpec((tm, d), lambda i: (i, 0)), weight_spec, weight_spec, weight_spec],
        out_specs=[pl.BlockSpec((d, tm), lambda i: (0, i)),
                   pl.BlockSpec((tm, d), lambda i: (i, 0)),
                   pl.BlockSpec((d, tm), lambda i: (0, i))],
        out_shape=[transposed, jax.ShapeDtypeStruct((t, d), _BF16), transposed],
        compiler_params=pltpu.CompilerParams(dimension_semantics=("arbitrary",),
                                             vmem_limit_bytes=VMEM_LIMIT_BYTES),
        name="qkv_proj",
    )(h2d, wq_t, wk, wv_t)


def _post_attention_kernel(o_ref, h_ref, wo_ref, g1_ref, b1_ref, wu_ref, wd_ref, g2_ref, b2_ref,
                           out_ref, *, alpha, d_ff, fc, sub):
    mixed, expanded = {}, {}

    def project(s):
        mixed[s] = jnp.dot(o_ref[s * sub:(s + 1) * sub, :], wo_ref[...],
                           preferred_element_type=_F32)

    def mix_channels(s):
        h1 = _layer_norm(alpha * h_ref[s * sub:(s + 1) * sub, :] + mixed.pop(s),
                         g1_ref[...], b1_ref[...])
        hb = h1.astype(_BF16)
        acc = alpha * h1
        for c in range(d_ff // fc):
            u = jnp.dot(hb, wu_ref[:, c * fc:(c + 1) * fc], preferred_element_type=_F32)
            u = jnp.square(jnp.maximum(u, 0.0)).astype(_BF16)
            acc = acc + jnp.dot(u, wd_ref[c * fc:(c + 1) * fc, :], preferred_element_type=_F32)
        expanded[s] = acc

    def normalize(s):
        out_ref[s * sub:(s + 1) * sub, :] = _layer_norm(expanded.pop(s), g2_ref[...], b2_ref[...])

    _software_pipeline(o_ref.shape[0] // sub, [project, mix_channels, normalize], 1)


def _post_attention(o2d, h2d, layer, w_o, g1, b1, w_up, w_down, g2, b2, alpha):
    t, d = h2d.shape
    d_ff = w_up.shape[2]
    tm = min(MLP_TOKEN_TILE, t)
    sub = min(EPILOGUE_ROW_SUBTILE, tm)
    fc = min(FF_CHUNK, d_ff)
    assert t % tm == 0 and tm % sub == 0 and d_ff % fc == 0
    rows = pl.BlockSpec((tm, d), lambda i: (i, 0))
    vec = pl.BlockSpec((1, d), lambda i: (0, 0))

    def resident(shape):
        return _layer_weight_spec(layer, shape)

    return pl.pallas_call(
        functools.partial(_post_attention_kernel, alpha=alpha, d_ff=d_ff, fc=fc, sub=sub),
        grid=(t // tm,),
        in_specs=[rows, rows, resident((d, d)), vec, vec,
                  resident((d, d_ff)), resident((d_ff, d)), vec, vec],
        out_specs=rows,
        out_shape=jax.ShapeDtypeStruct((t, d), _F32),
        compiler_params=pltpu.CompilerParams(dimension_semantics=("arbitrary",),
                                             vmem_limit_bytes=POST_ATTENTION_VMEM_LIMIT_BYTES),
        name="oproj_mlp_ln",
    )(o2d, h2d, w_o, g1.reshape(1, d), b1.reshape(1, d), w_up, w_down,
      g2.reshape(1, d), b2.reshape(1, d))


def _bias_kernel(tab_ref, bias_ref, cfar_ref):
    h = pl.program_id(0)
    w = MOBA_BLOCK
    key = lax.broadcasted_iota(jnp.int32, (w, w), 0)
    qry = lax.broadcasted_iota(jnp.int32, (w, w), 1)
    far = tab_ref[REL_BUCKETS - 1, h] * LOG2E
    for d in range(2):
        dist = qry - key + d * w
        val = jnp.full((w, w), far, _F32)
        for bkt in range(REL_BUCKETS - 2, -1, -1):
            val = jnp.where(dist <= _BUCKET_UB[bkt], tab_ref[bkt, h] * LOG2E, val)
        if d == 0:
            val = jnp.where(dist >= 0, val, _NEG_INF)
        bias_ref[0, d] = val
    cfar_ref[0] = jnp.full((8, w), far, _F32)


def _bias_tiles(rel_bias):
    n_heads = rel_bias.shape[1]
    w = MOBA_BLOCK
    return pl.pallas_call(
        _bias_kernel,
        grid=(n_heads,),
        in_specs=[pl.BlockSpec(memory_space=pltpu.SMEM)],
        out_specs=[pl.BlockSpec((1, 2, w, w), lambda h: (h, 0, 0, 0)),
                   pl.BlockSpec((1, 8, w), lambda h: (h, 0, 0))],
        out_shape=[jax.ShapeDtypeStruct((n_heads, 2, w, w), _F32),
                   jax.ShapeDtypeStruct((n_heads, 8, w), _F32)],
        compiler_params=pltpu.CompilerParams(dimension_semantics=("arbitrary",)),
        name="t5_bias_tiles",
    )(rel_bias.astype(_F32))


def _head_rows(h):
    row = lax.broadcasted_iota(jnp.int32, (LANES, MOBA_BLOCK), 0)
    return (row // HEAD_DIM) == h


def _store_head_queries(q_ref, qT_ref, nb):
    w = MOBA_BLOCK
    for h in range(HEADS_PER_STEP):
        mine = _head_rows(h)
        for i in range(nb):
            q_t = q_ref[:, i * w:(i + 1) * w]
            qT_ref[h, i] = jnp.where(mine, q_t, jnp.zeros_like(q_t))


def _head_slice(h):
    return slice(h * HEAD_DIM, (h + 1) * HEAD_DIM)


def _store_query_block(o_ref, oT_ref, h, i, result):
    w = MOBA_BLOCK
    oT_ref[i, _head_slice(h), :] = result
    if h == HEADS_PER_STEP - 1:
        o_ref[i * w:(i + 1) * w, :] = oT_ref[i].T.astype(_BF16)


def _causal_tiles(nb, seqs, newest=None):
    tiles = []
    for base in range(0, seqs * nb, nb):
        for i in range(nb):
            keys = list(range(i, -1, -1))
            if newest is not None:
                keys = keys[:newest] if newest > 0 else keys[-newest:]
            tiles += [(h, base + i, base + j) for j in keys for h in range(HEADS_PER_STEP)]
    return tiles


def _software_pipeline(n_items, stages, lookahead):
    n_stages = len(stages)
    for step in range(n_items + (n_stages - 1) * lookahead):
        for s, stage in enumerate(stages):
            t = step - s * lookahead
            if 0 <= t < n_items:
                stage(t)


def _moba_kernel(q_ref, k_ref, v_ref, bias_ref, cfar_ref, o_ref,
                 qT_ref, oT_ref, *, nb, seqs, topk):
    w = MOBA_BLOCK
    _store_head_queries(q_ref, qT_ref, seqs * nb)

    ones_rows = jnp.where(lax.broadcasted_iota(jnp.int32, (BF16_ROW_TILE, w), 0) == 0,
                          1.0, 0.0).astype(_BF16)

    def values_and_ones(h, j):
        return jnp.concatenate([v_ref[_head_slice(h), j * w:(j + 1) * w], ones_rows], axis=0)

    jidx = lax.broadcasted_iota(jnp.int32, (8, w), 0)
    addrows = {}
    for base in range(0, seqs * nb, nb):
        rows = [jnp.mean(k_ref[(base + j) * w:(base + j + 1) * w, :].astype(_F32),
                         axis=0, keepdims=True) for j in range(nb)]
        kmean = jnp.concatenate(rows + [jnp.zeros((16 - nb, LANES), _F32)], axis=0)
        p0 = kmean.astype(_BF16)
        r1 = kmean - p0.astype(_F32)
        p1 = r1.astype(_BF16)
        p2 = (r1 - p1.astype(_F32)).astype(_BF16)
        kparts = jnp.concatenate([p0, p1, p2], axis=0)
        for h in range(HEADS_PER_STEP):
            cfar = cfar_ref[h][0:1, :]
            for i in range(1, nb):
                g3 = jnp.dot(kparts, qT_ref[h, base + i], preferred_element_type=_F32)
                gate = g3[0:8] + g3[16:24] + g3[32:40]
                cnt = jnp.zeros((8, w), jnp.int32)
                for jp in range(i):
                    rowv = gate[jp:jp + 1, :]
                    beats = (rowv > gate) | ((rowv == gate) & (jp < jidx))
                    cnt = cnt + jnp.where(beats, 1, 0)
                sel = (cnt < topk) & (jidx < i)
                addrows[h, base + i] = jnp.where(sel, jnp.where(jidx < i - 1, cfar, 0.0),
                                                 _NEG_INF)

    tiles = _causal_tiles(nb, seqs)
    scores = {}
    state = {}

    def issue_scores(t):
        h, i, j = tiles[t]
        s = jnp.dot(k_ref[j * w:(j + 1) * w, :], qT_ref[h, i], preferred_element_type=_F32)
        if j >= i - 1:
            s = s + bias_ref[h, i - j]
        scores[t] = s

    def softmax_accumulate(t):
        h, i, j = tiles[t]
        s = scores.pop(t)
        mt = jnp.max(s, axis=0, keepdims=True)
        if j == i:
            m = mt
            p = jnp.exp2(s - m)
            acc = jnp.dot(values_and_ones(h, j), p.astype(_BF16), preferred_element_type=_F32)
        else:
            arow = addrows[h, i][j % nb:j % nb + 1, :]
            m_old, acc_old = state[h]
            m = jnp.maximum(m_old, mt + arow)
            a = jnp.exp2(m_old - m)
            p = jnp.exp2(s - (m - arow))
            acc = a * acc_old + jnp.dot(values_and_ones(h, j), p.astype(_BF16),
                                        preferred_element_type=_F32)
        state[h] = (m, acc)
        if j % nb == 0:
            denom = acc[HEAD_DIM:HEAD_DIM + 1, :]
            _store_query_block(o_ref, oT_ref, h, i, acc[0:HEAD_DIM, :] * (1.0 / denom))

    _software_pipeline(len(tiles), [issue_scores, softmax_accumulate], MOBA_LOOKAHEAD)


def _attention_call(kernel, q_t, k, v_t, seq, extra_inputs, extra_specs, scratch, name):
    d, t = q_t.shape
    w = MOBA_BLOCK
    assert seq % w == 0 and t % seq == 0 and d % LANES == 0
    nb = seq // w
    assert nb <= 8
    seqs = SEQS_PER_STEP if (t // seq) % SEQS_PER_STEP == 0 else 1
    rows = seqs * seq
    transposed_spec = pl.BlockSpec((LANES, rows), lambda hp, bi: (hp, bi))
    token_major_spec = pl.BlockSpec((rows, LANES), lambda hp, bi: (bi, hp))
    return pl.pallas_call(
        functools.partial(kernel, nb=nb, seqs=seqs),
        grid=(d // LANES, t // rows),
        in_specs=[transposed_spec, token_major_spec, transposed_spec] + extra_specs,
        out_specs=token_major_spec,
        out_shape=jax.ShapeDtypeStruct((t, d), _BF16),
        scratch_shapes=[pltpu.VMEM((HEADS_PER_STEP, seqs * nb, LANES, w), _BF16)]
                       + [pltpu.VMEM(shape(seqs), dt) for shape, dt in scratch]
                       + [pltpu.VMEM((seqs * nb, LANES, w), _F32)],
        compiler_params=pltpu.CompilerParams(
            dimension_semantics=("arbitrary", "arbitrary"),
            vmem_limit_bytes=VMEM_LIMIT_BYTES),
        name=name,
    )(q_t, k, v_t, *extra_inputs)


def _moba_attention(q_t, k, v_t, seq, bias_tiles, cfar):
    w = MOBA_BLOCK
    nb = seq // w
    topk = min(MOBA_TOPK, nb)
    extra_specs = [
        pl.BlockSpec((HEADS_PER_STEP, 2, w, w), lambda hp, bi: (hp, 0, 0, 0)),
        pl.BlockSpec((HEADS_PER_STEP, 8, w), lambda hp, bi: (hp, 0, 0)),
    ]
    return _attention_call(functools.partial(_moba_kernel, topk=topk), q_t, k, v_t, seq,
                           [bias_tiles, cfar], extra_specs, [], "moba_attention")


def _sb_kernel(q_ref, k_ref, v_ref, o_ref, qT_ref, r_ref, oT_ref, *, nb, seqs):
    w = MOBA_BLOCK
    _store_head_queries(q_ref, qT_ref, seqs * nb)

    key = lax.broadcasted_iota(jnp.int32, (w, w), 0)
    col = lax.broadcasted_iota(jnp.int32, (w, w), 1)
    causal = key < col
    upper = jnp.where(col > key, 1.0, 0.0).astype(_BF16)

    def run_tiles(tiles, resume):
        logits, partial, state, r_final = {}, {}, {}, {}
        first_key = {}
        for h, i, j in tiles:
            first_key.setdefault((h, i), j)
        last_key = {(h, i): j for h, i, j in tiles}

        def issue_logits(t):
            h, i, j = tiles[t]
            logits[t] = jnp.dot(k_ref[j * w:(j + 1) * w, :], qT_ref[h, i],
                                preferred_element_type=_F32)

        def issue_cumsum(t):
            h, i, j = tiles[t]
            z = logits.pop(t)
            sp = jnp.maximum(z, jnp.log(1.0 + jnp.exp2(jnp.minimum(z, EXP2_SAFE_MAX))) * LOG2E)
            if j == i:
                sp = jnp.where(causal, sp, 0.0)
            after = jnp.dot(upper, sp.astype(_BF16), preferred_element_type=_F32)
            partial[t] = (z - sp, after, sp[0:1, :])

        def weigh_values(t):
            h, i, j = tiles[t]
            log_sig, after, sp_first = partial.pop(t)
            a = jnp.exp2(log_sig - after)
            if j == i:
                a = jnp.where(causal, a, 0.0)
            pv = jnp.dot(v_ref[_head_slice(h), j * w:(j + 1) * w], a.astype(_BF16),
                         preferred_element_type=_F32)
            tile_total = after[0:1, :] + sp_first
            if j == i:
                acc, r = pv, tile_total
            else:
                if j == first_key[h, i]:
                    acc_old, r_old = oT_ref[i, _head_slice(h), :], r_ref[h, i]
                else:
                    acc_old, r_old = state[h]
                acc = acc_old + pv * jnp.exp2(-r_old)
                r = r_old + tile_total
            state[h] = (acc, r)
            if j == last_key[h, i]:
                _store_query_block(o_ref, oT_ref, h, i, acc)
                r_final[h, i] = r
                if not resume:
                    r_ref[h, i] = r

        _software_pipeline(len(tiles), [issue_logits, issue_cumsum, weigh_values], SB_LOOKAHEAD)
        return r_final

    r_near = run_tiles(_causal_tiles(nb, seqs, newest=SB_NEAR_TILES), resume=False)
    far_tiles = _causal_tiles(nb, seqs, newest=-SB_NEAR_TILES)
    if far_tiles:
        r_min = None
        for h, i in sorted({(h, i) for h, i, _ in far_tiles}):
            r_min = r_near[h, i] if r_min is None else jnp.minimum(r_min, r_near[h, i])

        @pl.when(jnp.min(r_min) <= SB_SKIP_LOG2)
        def _():
            run_tiles(far_tiles, resume=True)


def _sb_attention(q_t, k, v_t, seq):
    nb = seq // MOBA_BLOCK
    r_scratch = (lambda seqs: (HEADS_PER_STEP, seqs * nb, 1, MOBA_BLOCK), _F32)
    return _attention_call(_sb_kernel, q_t, k, v_t, seq, [], [], [r_scratch],
                           "stick_breaking_attention")


def kernel(x, rel_bias, w_qkv, w_o, ln_mix_g, ln_mix_b, w_up, w_down, ln_ffn_g, ln_ffn_b):
    b, s, d = x.shape
    depth = w_qkv.shape[0]
    alpha = (2.0 * depth) ** 0.25
    assert d % (HEADS_PER_STEP * HEAD_DIM) == 0
    bias_tiles, cfar = _bias_tiles(rel_bias)
    qkv_weights = _split_qkv_weights(w_qkv)
    w_o, w_up, w_down = w_o.astype(_BF16), w_up.astype(_BF16), w_down.astype(_BF16)
    h = x.reshape(b * s, d)
    for i in range(depth):
        q_t, k, v_t = _qkv_proj(h, qkv_weights, i)
        if i % 2 == 0:
            o = _moba_attention(q_t, k, v_t, s, bias_tiles, cfar)
        else:
            o = _sb_attention(q_t, k, v_t, s)
        h = _post_attention(o, h, i, w_o, ln_mix_g[i], ln_mix_b[i],
                            w_up, w_down, ln_ffn_g[i], ln_ffn_b[i], alpha)
    return h.reshape(b, s, d)
```
